```python
import jax, jax.numpy as jnp
from jax import lax
import numpy as np

D_MODEL = 1024
BATCH = 8
SEQ = 4096
DEPTH = 4

CHUNK = 64
PLE_DIM = 256
CONV_CH = 256
CONV_WIDTH = 31
GLA_HEADS = 6
GLA_DK = 32
GLA_DV = 64
GLA_GATE_RANK = 16
GLA_GATE_TAU = 16.0
FOX_HEADS = 6
FOX_DH = 64
FOX_BLOCK = 128
MIX_WIDTH = CONV_CH + GLA_HEADS * GLA_DV + FOX_HEADS * FOX_DH
PEER_HEADS = 8
PEER_NKEYS = 128
PEER_N = PEER_NKEYS * PEER_NKEYS
PEER_QDIM = 256
PEER_HALF = PEER_QDIM // 2
PEER_TOPK = 16
PEER_TOKEN_BLOCK = 128
DEEPNORM_ALPHA = (2.0 * DEPTH) ** 0.25
DEEPNORM_BETA = (8.0 * DEPTH) ** -0.25
LN_EPS = 1e-5
IN_SIZES = (CONV_CH, CONV_CH,
            GLA_HEADS * GLA_DK, GLA_HEADS * GLA_DK,
            GLA_HEADS * GLA_DV, GLA_HEADS * GLA_DV,
            GLA_GATE_RANK,
            FOX_HEADS * FOX_DH, FOX_HEADS * FOX_DH, FOX_HEADS * FOX_DH,
            FOX_HEADS)
IN_COLS = sum(IN_SIZES)
IN_SPLITS = tuple(int(s) for s in np.cumsum(IN_SIZES)[:-1])

kernel_name = "hybrid_conv_gla_fox_peer_deepnorm"


def layer_norm(x, g, b):
    xf = x.astype(jnp.float32)
    mu = jnp.mean(xf, axis=-1, keepdims=True)
    var = jnp.mean(jnp.square(xf - mu), axis=-1, keepdims=True)
    y = (xf - mu) * lax.rsqrt(var + LN_EPS)
    return (y * g.astype(jnp.float32) + b.astype(jnp.float32)).astype(x.dtype)


def conv_module(a_val, a_gate, w, bconv, g, b):
    u = a_val * jax.nn.sigmoid(a_gate)
    y = lax.conv_general_dilated(
        u, w[:, None, :].astype(u.dtype), window_strides=(1,),
        padding=[(CONV_WIDTH - 1, 0)],
        dimension_numbers=('NWC', 'WIO', 'NWC'),
        feature_group_count=CONV_CH) + bconv.astype(u.dtype)
    return jax.nn.silu(layer_norm(y, g, b))


def gla(q, k, v, g_out, lr, w2, b2, norm_g):
    bsz, s = q.shape[:2]
    n_chunks = s // CHUNK
    f32 = jnp.float32
    q = q.reshape(bsz, s, GLA_HEADS, GLA_DK).astype(f32) * (GLA_DK ** -0.5)
    k = k.reshape(bsz, s, GLA_HEADS, GLA_DK).astype(f32)
    v = v.reshape(bsz, s, GLA_HEADS, GLA_DV).astype(f32)
    gk = jax.nn.log_sigmoid((lr @ w2 + b2).astype(f32)) / GLA_GATE_TAU
    gk = gk.reshape(bsz, s, GLA_HEADS, GLA_DK)

    def to_chunks(t):
        return t.reshape(bsz, n_chunks, CHUNK, GLA_HEADS, t.shape[-1]).transpose(1, 0, 3, 2, 4)

    mask = jnp.tril(jnp.ones((CHUNK, CHUNK), dtype=bool))[:, :, None]

    def step(state, inp):
        qc, kc, vc, gc = inp
        cum = jnp.cumsum(gc, axis=2)
        inter = jnp.einsum('bhld,bhde->bhle', qc * jnp.exp(cum), state)
        diff = cum[:, :, :, None, :] - cum[:, :, None, :, :]
        decay = jnp.exp(jnp.where(mask, diff, -jnp.inf))
        attn = jnp.einsum('bhtd,bhsd,bhtsd->bhts', qc, kc, decay)
        intra = jnp.einsum('bhts,bhse->bhte', attn, vc)
        last = cum[:, :, -1:, :]
        k_dec = kc * jnp.exp(last - cum)
        new_state = state * jnp.exp(last[:, :, 0, :])[..., None] + \
            jnp.einsum('bhld,bhle->bhde', k_dec, vc)
        return new_state, inter + intra

    state0 = jnp.zeros((bsz, GLA_HEADS, GLA_DK, GLA_DV), f32)
    _, o = lax.scan(step, state0, (to_chunks(q), to_chunks(k), to_chunks(v), to_chunks(gk)))
    o = o.transpose(1, 0, 3, 2, 4).reshape(bsz, s, GLA_HEADS, GLA_DV)
    o = o * lax.rsqrt(jnp.mean(jnp.square(o), axis=-1, keepdims=True) + LN_EPS)
    o = o * norm_g.astype(f32).reshape(GLA_HEADS, GLA_DV)
    o = o.reshape(bsz, s, GLA_HEADS * GLA_DV) * jax.nn.silu(g_out.astype(f32))
    return o.astype(g_out.dtype)


def fox(q, k, v, f_logit, b_f):
    bsz, s = q.shape[:2]
    q = q.reshape(bsz, s, FOX_HEADS, FOX_DH)
    k = k.reshape(bsz, s, FOX_HEADS, FOX_DH)
    v = v.reshape(bsz, s, FOX_HEADS, FOX_DH)
    log_f = jax.nn.log_sigmoid(f_logit.astype(jnp.float32) + b_f.astype(jnp.float32))
    c = jnp.cumsum(log_f, axis=1).transpose(0, 2, 1)
    scale = FOX_DH ** -0.5
    outs = []
    for i in range(s // FOX_BLOCK):
        q0, q1 = i * FOX_BLOCK, (i + 1) * FOX_BLOCK
        sc = jnp.einsum('bqhd,bkhd->bhqk', q[:, q0:q1], k[:, :q1]).astype(jnp.float32) * scale
        sc = sc + c[:, :, q0:q1, None] - c[:, :, None, :q1]
        causal = jnp.arange(q0, q1)[:, None] >= jnp.arange(q1)[None, :]
        pr = jax.nn.softmax(jnp.where(causal, sc, -jnp.inf), axis=-1)
        outs.append(jnp.einsum('bhqk,bkhd->bqhd', pr.astype(v.dtype), v[:, :q1]))
    return jnp.concatenate(outs, axis=1).reshape(bsz, s, FOX_HEADS * FOX_DH)


def peer(x, wq, keys, u, v):
    bsz, s, d = x.shape
    qp = (x @ wq).reshape(bsz, s, PEER_HEADS, 2, PEER_HALF)
    sc = jnp.einsum('bshcd,hcnd->bshcn', qp, keys).astype(jnp.float32)
    sv, si = lax.top_k(sc, PEER_TOPK)
    cand = (sv[..., 0, :, None] + sv[..., 1, None, :]).reshape(bsz, s, PEER_HEADS, PEER_TOPK * PEER_TOPK)
    cv, ci = lax.top_k(cand, PEER_TOPK)
    i1 = jnp.take_along_axis(si[..., 0, :], ci // PEER_TOPK, axis=-1)
    i2 = jnp.take_along_axis(si[..., 1, :], ci % PEER_TOPK, axis=-1)
    n_tok = bsz * s
    n_blk = n_tok // PEER_TOKEN_BLOCK
    eidx = (i1 * PEER_NKEYS + i2).reshape(n_blk, PEER_TOKEN_BLOCK, PEER_HEADS * PEER_TOPK)
    gates = jax.nn.softmax(cv, axis=-1).reshape(n_blk, PEER_TOKEN_BLOCK, PEER_HEADS * PEER_TOPK)
    xt = x.reshape(n_blk, PEER_TOKEN_BLOCK, d)

    def block(args):
        xb, ib, gb = args
        h = jax.nn.gelu(jnp.einsum('td,tkd->tk', xb, u[ib]).astype(jnp.float32))
        return jnp.einsum('tk,tkd->td', (gb * h).astype(xb.dtype), v[ib])

    out = lax.map(block, (xt, eidx, gates))
    return out.reshape(bsz, s, d)


def setup_inputs(seed: int = 0) -> dict:
    key = jax.random.key(seed)
    ks = jax.random.split(key, 24)
    n = lambda k, shape: jax.random.normal(k, shape, jnp.float32)
    L, D = DEPTH, D_MODEL
    return {
        "x": n(ks[0], (BATCH, SEQ, D)),
        "p": n(ks[1], (DEPTH, BATCH, SEQ, PLE_DIM)),
        "w_in": n(ks[2], (L, D, IN_COLS)) * D ** -0.5,
        "conv_w": n(ks[3], (L, CONV_WIDTH, CONV_CH)) * CONV_WIDTH ** -0.5,
        "conv_b": 0.02 * n(ks[4], (L, CONV_CH)),
        "conv_ln_g": 1.0 + 0.02 * n(ks[5], (L, CONV_CH)),
        "conv_ln_b": 0.02 * n(ks[6], (L, CONV_CH)),
        "gla_gate_w": n(ks[7], (L, GLA_GATE_RANK, GLA_HEADS * GLA_DK)) * GLA_GATE_RANK ** -0.5,
        "gla_gate_b": 0.1 * n(ks[8], (L, GLA_HEADS * GLA_DK)),
        "gla_norm_g": 1.0 + 0.02 * n(ks[9], (L, GLA_HEADS * GLA_DV)),
        "fox_forget_b": 3.0 + 0.5 * n(ks[10], (L, FOX_HEADS)),
        "w_out": n(ks[11], (L, MIX_WIDTH, D)) * MIX_WIDTH ** -0.5 * DEEPNORM_BETA,
        "ln1_g": 1.0 + 0.02 * n(ks[12], (L, D)),
        "ln1_b": 0.02 * n(ks[13], (L, D)),
        "peer_wq": n(ks[14], (L, D, PEER_HEADS * PEER_QDIM)) * D ** -0.5,
        "peer_keys": n(ks[15], (L, PEER_HEADS, 2, PEER_NKEYS, PEER_HALF)) * PEER_HALF ** -0.5,
        "peer_u": n(ks[16], (L, PEER_N, D)) * D ** -0.5,
        "peer_v": n(ks[17], (L, PEER_N, D)) * DEEPNORM_BETA * PEER_HEADS ** -0.5,
        "ple_w": n(ks[18], (L, PLE_DIM, D)) * PLE_DIM ** -0.5 * DEEPNORM_BETA,
        "ple_gw": n(ks[19], (L, D, D)) * D ** -0.5,
        "ple_gb": 0.02 * n(ks[20], (L, D)),
        "ln2_g": 1.0 + 0.02 * n(ks[21], (L, D)),
        "ln2_b": 0.02 * n(ks[22], (L, D)),
    }


def reference(x, p, w_in, conv_w, conv_b, conv_ln_g, conv_ln_b, gla_gate_w, gla_gate_b, gla_norm_g,
              fox_forget_b, w_out, ln1_g, ln1_b, peer_wq, peer_keys, peer_u, peer_v,
              ple_w, ple_gw, ple_gb, ln2_g, ln2_b):
    for i in range(DEPTH):
        h = x @ w_in[i]
        (a_val, a_gate, b_q, b_k, b_v, b_g, b_lr, c_q, c_k, c_v, c_f) = jnp.split(h, IN_SPLITS, axis=-1)
        a_out = conv_module(a_val, a_gate, conv_w[i], conv_b[i], conv_ln_g[i], conv_ln_b[i])
        b_out = gla(b_q, b_k, b_v, b_g, b_lr, gla_gate_w[i], gla_gate_b[i], gla_norm_g[i])
        c_out = fox(c_q, c_k, c_v, c_f, fox_forget_b[i])
        mix = jnp.concatenate([a_out, b_out, c_out], axis=-1) @ w_out[i]
        x = layer_norm(DEEPNORM_ALPHA * x + mix, ln1_g[i], ln1_b[i])
        r = DEEPNORM_ALPHA * x + peer(x, peer_wq[i], peer_keys[i], peer_u[i], peer_v[i])
        r = r + jax.nn.sigmoid(r @ ple_gw[i] + ple_gb[i]) * (p[i] @ ple_w[i])
        x = layer_norm(r, ln2_g[i], ln2_b[i])
    return x
```

```python
import functools

import jax
import jax.numpy as jnp
from jax import lax
from jax.experimental import pallas as pl
from jax.experimental.pallas import tpu as pltpu

F32, BF16, I32 = jnp.float32, jnp.bfloat16, jnp.int32
HIGHEST = lax.Precision.HIGHEST

D_MODEL = 1024
DEPTH = 4
CHUNK = 64
PLE_DIM = 256
CONV_CH = 256
CONV_WIDTH = 31
GLA_HEADS = 6
GLA_DK = 32
GLA_DV = 64
GLA_GATE_RANK = 16
GLA_GATE_TAU = 16.0
FOX_HEADS = 6
FOX_DH = 64
PEER_HEADS = 8
PEER_NKEYS = 128
PEER_N = PEER_NKEYS * PEER_NKEYS
PEER_HALF = 128
PEER_TOPK = 16
DEEPNORM_ALPHA = (2.0 * DEPTH) ** 0.25
LN_EPS = 1e-5

LANES = 128
SUBLANES = 8
VMEM_LIMIT = 48 * 1024 * 1024

GLA_DKP = 256
GLA_VW = GLA_HEADS * GLA_DV
FOX_W = FOX_HEADS * FOX_DH
A_W = 2 * CONV_CH
B_W = 2 * GLA_DKP + 2 * GLA_VW + LANES
C_W = 3 * FOX_W
W_IN_COLS = A_W + B_W + C_W

TN_PROJ = 512
TS_CONV = 512
TQ_FOX = 256
TN_ROUTE = 256
TN_PEER = 64
PEER_WORDS = 4
PEER_STAGE = PEER_HEADS * PEER_TOPK * PEER_WORDS


def _cparams(*sem):
    return pltpu.CompilerParams(dimension_semantics=sem, vmem_limit_bytes=VMEM_LIMIT)


def _nt(a, b, precision=None):
    return lax.dot_general(a, b, (((1,), (1,)), ((), ())),
                           preferred_element_type=F32, precision=precision)


def _mm(a, b, precision=None):
    return jnp.dot(a, b, preferred_element_type=F32, precision=precision)


def _layer_norm(r, g, b):
    mu = jnp.mean(r, axis=-1, keepdims=True)
    d = r - mu
    var = jnp.mean(d * d, axis=-1, keepdims=True)
    return d * lax.rsqrt(var + LN_EPS) * g + b


def _log_sigmoid(x):
    return jnp.minimum(x, 0.0) - jnp.log(1.0 + jnp.exp(-jnp.abs(x)))


def _sigmoid(x):
    return 1.0 / (1.0 + jnp.exp(-x))


def _div_pow2(i, d):
    assert d & (d - 1) == 0
    return lax.shift_right_logical(i, jnp.int32(d.bit_length() - 1))


def _inproj_kernel(x_ref, w_ref, wf_ref, ha_ref, hb_ref, hc_ref, ft_ref):
    x = x_ref[...]
    xb = x.astype(BF16)
    ha_ref[...] = _mm(xb, w_ref[:, 0:A_W])
    hb_ref[...] = _mm(xb, w_ref[:, A_W:A_W + B_W])
    hc_ref[...] = _mm(xb, w_ref[:, A_W + B_W:W_IN_COLS]).astype(BF16)
    ft_ref[...] = _nt(wf_ref[...], x, precision=HIGHEST)


def _inproj(x, wp, wft):
    n = x.shape[0]
    tn = min(TN_PROJ, n)
    return pl.pallas_call(
        _inproj_kernel,
        grid=(n // tn,),
        in_specs=[pl.BlockSpec((tn, D_MODEL), lambda i: (i, 0)),
                  pl.BlockSpec((D_MODEL, W_IN_COLS), lambda i: (0, 0)),
                  pl.BlockSpec((SUBLANES, D_MODEL), lambda i: (0, 0))],
        out_specs=[pl.BlockSpec((tn, A_W), lambda i: (i, 0)),
                   pl.BlockSpec((tn, B_W), lambda i: (i, 0)),
                   pl.BlockSpec((tn, C_W), lambda i: (i, 0)),
                   pl.BlockSpec((SUBLANES, tn), lambda i: (0, i))],
        out_shape=[jax.ShapeDtypeStruct((n, A_W), F32),
                   jax.ShapeDtypeStruct((n, B_W), F32),
                   jax.ShapeDtypeStruct((n, C_W), BF16),
                   jax.ShapeDtypeStruct((SUBLANES, n), F32)],
        compiler_params=_cparams("arbitrary"),
        name="inproj",
    )(x, wp, wft)


CONV_HALO = 32
CONV_SUB = 64


def _conv_kernel(ha_ref, w_ref, b_ref, g_ref, beta_ref, o_ref, ubuf):
    j = pl.program_id(1)
    ts = ha_ref.shape[0]

    @pl.when(j == 0)
    def _():
        ubuf[0:CONV_HALO, :] = jnp.zeros((CONV_HALO, CONV_CH), F32)

    @pl.when(j > 0)
    def _():
        ubuf[0:CONV_HALO, :] = ubuf[ts:ts + CONV_HALO, :]

    ubuf[CONV_HALO:CONV_HALO + ts, :] = ha_ref[:, 0:CONV_CH] * _sigmoid(ha_ref[:, CONV_CH:2 * CONV_CH])

    off = CONV_HALO - (CONV_WIDTH - 1)

    for base in range(0, ts, CONV_SUB):
        acc = jnp.zeros((CONV_SUB, CONV_CH), F32) + b_ref[...]
        for t in range(CONV_WIDTH):
            acc = acc + w_ref[t:t + 1, :] * ubuf[base + off + t:base + off + t + CONV_SUB, :]
        y = _layer_norm(acc, g_ref[...], beta_ref[...])
        o_ref[base:base + CONV_SUB, :] = (y * _sigmoid(y)).astype(BF16)


def _conv_module(ha, w, b, g, beta, bsz, seq):
    ts = min(TS_CONV, seq)
    nb = seq // ts
    return pl.pallas_call(
        _conv_kernel,
        grid=(bsz, nb),
        in_specs=[pl.BlockSpec((ts, A_W), lambda bi, j: (bi * nb + j, 0)),
                  pl.BlockSpec((CONV_WIDTH, CONV_CH), lambda bi, j: (0, 0)),
                  pl.BlockSpec((1, CONV_CH), lambda bi, j: (0, 0)),
                  pl.BlockSpec((1, CONV_CH), lambda bi, j: (0, 0)),
                  pl.BlockSpec((1, CONV_CH), lambda bi, j: (0, 0))],
        out_specs=pl.BlockSpec((ts, CONV_CH), lambda bi, j: (bi * nb + j, 0)),
        out_shape=jax.ShapeDtypeStruct((bsz * seq, CONV_CH), BF16),
        scratch_shapes=[pltpu.VMEM((CONV_HALO + ts, CONV_CH), F32)],
        compiler_params=_cparams("arbitrary", "arbitrary"),
        name="conv_module",
    )(ha, w, b, g, beta)


def _gla_kernel(hb_ref, w2_ref, b2_ref, ng_ref, o_ref, st_ref, cum_ref, q_sc, abuf):
    L = CHUNK

    @pl.when(pl.program_id(1) == 0)
    def _():
        st_ref[...] = jnp.zeros(st_ref.shape, F32)

    q = hb_ref[:, 0:GLA_DKP] * (GLA_DK ** -0.5)
    k = hb_ref[:, GLA_DKP:2 * GLA_DKP]
    v = hb_ref[:, 2 * GLA_DKP:2 * GLA_DKP + GLA_VW]
    g_out = hb_ref[:, 2 * GLA_DKP + GLA_VW:2 * GLA_DKP + 2 * GLA_VW]
    lr = hb_ref[:, 2 * GLA_DKP + 2 * GLA_VW:B_W]

    z = _mm(lr.astype(BF16), w2_ref[...]) + b2_ref[...]
    gk = _log_sigmoid(z) * (1.0 / GLA_GATE_TAU)

    r_i = lax.broadcasted_iota(I32, (L, L), 0)
    c_i = lax.broadcasted_iota(I32, (L, L), 1)
    causal = r_i >= c_i
    cum = _mm(causal.astype(F32), gk, precision=HIGHEST)
    cum_ref[...] = cum
    q_sc[...] = q

    hm_r = lax.broadcasted_iota(I32, (SUBLANES, GLA_DKP), 0)
    hm_c = lax.broadcasted_iota(I32, (SUBLANES, GLA_DKP), 1)
    head_sel = _div_pow2(hm_c, GLA_DK) == hm_r

    vb = v.astype(BF16)

    def row(t, carry):
        cum_t = cum_ref[pl.ds(t, 1), :]
        q_t = q_sc[pl.ds(t, 1), :]
        w = k * jnp.exp(jnp.minimum(cum_t - cum, 0.0))
        qm = jnp.where(head_sel, q_t, 0.0)
        abuf[pl.ds(pl.multiple_of(t * SUBLANES, SUBLANES), SUBLANES), :] = _nt(qm.astype(BF16), w.astype(BF16))
        return carry

    lax.fori_loop(0, L, row, 0)

    st = st_ref[...]
    inter = _nt((q * jnp.exp(cum)).astype(BF16), st.astype(BF16))

    lane_head = _div_pow2(lax.broadcasted_iota(I32, (L, GLA_VW), 1), GLA_DV)
    o = inter
    for h in range(GLA_HEADS):
        a_h = abuf[pl.ds(h, L, stride=SUBLANES), :]
        a_h = jnp.where(causal, a_h, 0.0)
        p = _mm(a_h.astype(BF16), vb)
        o = o + jnp.where(lane_head == h, p, 0.0)

    last = cum[L - 1:L, :]
    kdec = k * jnp.exp(last - cum)
    upd = lax.dot_general(vb, kdec.astype(BF16), (((0,), (0,)), ((), ())), preferred_element_type=F32)
    s_r = _div_pow2(lax.broadcasted_iota(I32, st_ref.shape, 0), GLA_DV)
    s_c = _div_pow2(lax.broadcasted_iota(I32, st_ref.shape, 1), GLA_DK)
    st_ref[...] = st * jnp.exp(last) + jnp.where(s_r == s_c, upd, 0.0)

    o2 = o * o
    ms = jnp.zeros_like(o)
    for h in range(GLA_HEADS):
        sel = lane_head == h
        s = jnp.sum(jnp.where(sel, o2, 0.0), axis=-1, keepdims=True)
        ms = jnp.where(sel, s, ms)
    o = o * lax.rsqrt(ms * (1.0 / GLA_DV) + LN_EPS) * ng_ref[...]
    o_ref[...] = (o * (g_out * _sigmoid(g_out))).astype(BF16)


def _gla(hb, w2p, b2p, ng, bsz, seq):
    nc = seq // CHUNK
    return pl.pallas_call(
        _gla_kernel,
        grid=(bsz, nc),
        in_specs=[pl.BlockSpec((CHUNK, B_W), lambda bi, c: (bi * nc + c, 0)),
                  pl.BlockSpec((LANES, GLA_DKP), lambda bi, c: (0, 0)),
                  pl.BlockSpec((1, GLA_DKP), lambda bi, c: (0, 0)),
                  pl.BlockSpec((1, GLA_VW), lambda bi, c: (0, 0))],
        out_specs=pl.BlockSpec((CHUNK, GLA_VW), lambda bi, c: (bi * nc + c, 0)),
        out_shape=jax.ShapeDtypeStruct((bsz * seq, GLA_VW), BF16),
        scratch_shapes=[pltpu.VMEM((GLA_VW, GLA_DKP), F32),
                        pltpu.VMEM((CHUNK, GLA_DKP), F32),
                        pltpu.VMEM((CHUNK, GLA_DKP), F32),
                        pltpu.VMEM((CHUNK * SUBLANES, CHUNK), F32)],
        compiler_params=_cparams("arbitrary", "arbitrary"),
        name="gla",
    )(hb, w2p, b2p, ng)


def _fcum_kernel(ft_ref, bf_ref, c_ref):
    seq = ft_ref.shape[1]
    r_i = lax.broadcasted_iota(I32, (LANES, LANES), 0)
    c_i = lax.broadcasted_iota(I32, (LANES, LANES), 1)
    tri = (r_i <= c_i).astype(F32)
    carry = jnp.zeros((SUBLANES, 1), F32)
    for blk in range(seq // LANES):
        sl = slice(blk * LANES, (blk + 1) * LANES)
        ls = _log_sigmoid(ft_ref[:, sl] + bf_ref[...])
        cs = _mm(ls, tri, precision=HIGHEST) + carry
        c_ref[:, sl] = cs
        carry = cs[:, LANES - 1:LANES]


def _fox_cumsum(ft, bfp, bsz, seq):
    return pl.pallas_call(
        _fcum_kernel,
        grid=(bsz,),
        in_specs=[pl.BlockSpec((SUBLANES, seq), lambda bi: (0, bi)),
                  pl.BlockSpec((SUBLANES, 1), lambda bi: (0, 0))],
        out_specs=pl.BlockSpec((SUBLANES, seq), lambda bi: (0, bi)),
        out_shape=jax.ShapeDtypeStruct((SUBLANES, bsz * seq), F32),
        compiler_params=_cparams("arbitrary"),
        name="fox_cumsum",
    )(ft, bfp)


def _fox_kernel(q_ref, k_ref, v_ref, c_ref, o_ref):
    tq = q_ref.shape[0]
    qi = pl.program_id(2)
    lane = lax.broadcasted_iota(I32, (1, LANES), 1)
    first = lane < FOX_DH
    qs = q_ref[...] * (FOX_DH ** -0.5)
    r_i = lax.broadcasted_iota(I32, (tq, tq), 0)
    c_i = lax.broadcasted_iota(I32, (tq, tq), 1)
    causal = r_i >= c_i
    outs = []
    for j in range(2):
        qj = jnp.where(first if j == 0 else jnp.logical_not(first), qs, jnp.zeros_like(qs))
        c_q = c_ref[j, pl.ds(qi, 1), :]
        c_col = jnp.transpose(jnp.broadcast_to(c_q, (SUBLANES, tq)))[:, 0:1]

        def step(kb, carry, masked, qj=qj, c_col=c_col, j=j):
            m, l, acc = carry
            k0 = pl.multiple_of(kb * tq, tq)
            s = _nt(qj, k_ref[pl.ds(k0, tq), :])
            s = s + (c_col - c_ref[j, pl.ds(kb, 1), :])
            if masked:
                s = jnp.where(causal, s, -jnp.inf)
            m_new = jnp.maximum(m, jnp.max(s, axis=-1, keepdims=True))
            alpha = jnp.exp(m - m_new)
            p = jnp.exp(s - m_new)
            l = alpha * l + jnp.sum(p, axis=-1, keepdims=True)
            acc = alpha * acc + _mm(p.astype(BF16), v_ref[pl.ds(k0, tq), :])
            return m_new, l, acc

        carry = (jnp.full((tq, 1), -jnp.inf, F32), jnp.zeros((tq, 1), F32), jnp.zeros((tq, LANES), F32))
        carry = lax.fori_loop(0, qi, functools.partial(step, masked=False), carry)
        _, l, acc = step(qi, carry, True)
        outs.append(acc / l)
    o_ref[...] = jnp.where(first, outs[0], outs[1]).astype(BF16)


def _fox(hc, c4, bsz, seq):
    tq = min(TQ_FOX, seq)
    nq = seq // tq
    pairs = FOX_HEADS // 2
    return pl.pallas_call(
        _fox_kernel,
        grid=(bsz, pairs, nq),
        in_specs=[pl.BlockSpec((tq, LANES), lambda bi, p, i: (bi * nq + i, p)),
                  pl.BlockSpec((seq, LANES), lambda bi, p, i: (bi, pairs + p)),
                  pl.BlockSpec((seq, LANES), lambda bi, p, i: (bi, 2 * pairs + p)),
                  pl.BlockSpec((None, 2, nq, tq), lambda bi, p, i: (p, 0, bi, 0))],
        out_specs=pl.BlockSpec((tq, LANES), lambda bi, p, i: (bi * nq + i, p)),
        out_shape=jax.ShapeDtypeStruct((bsz * seq, FOX_W), BF16),
        compiler_params=_cparams("arbitrary", "arbitrary", "arbitrary"),
        name="fox_attention",
    )(hc, hc, hc, c4)


def _outproj_kernel(a_ref, b_ref, c_ref, x_ref, wa_ref, wb_ref, wc_ref, g_ref, beta_ref, o_ref):
    mix = _mm(a_ref[...], wa_ref[...]) + _mm(b_ref[...], wb_ref[...]) + _mm(c_ref[...], wc_ref[...])
    o_ref[...] = _layer_norm(DEEPNORM_ALPHA * x_ref[...] + mix, g_ref[...], beta_ref[...])


def _outproj(a, b, c, x, wa, wb, wc, g, beta):
    n = x.shape[0]
    tn = min(TN_PROJ, n)
    tok = lambda w: pl.BlockSpec((tn, w), lambda i: (i, 0))
    full = lambda r, w: pl.BlockSpec((r, w), lambda i: (0, 0))
    return pl.pallas_call(
        _outproj_kernel,
        grid=(n // tn,),
        in_specs=[tok(CONV_CH), tok(GLA_VW), tok(FOX_W), tok(D_MODEL),
                  full(CONV_CH, D_MODEL), full(GLA_VW, D_MODEL), full(FOX_W, D_MODEL),
                  full(1, D_MODEL), full(1, D_MODEL)],
        out_specs=tok(D_MODEL),
        out_shape=jax.ShapeDtypeStruct((n, D_MODEL), F32),
        compiler_params=_cparams("arbitrary"),
        name="outproj_ln",
    )(a, b, c, x, wa, wb, wc, g, beta)


def _topk_rows(sc, kk):
    nrow = sc.shape[0]
    rows = lax.broadcasted_iota(I32, sc.shape, 0).astype(F32)
    vals, idxs = [], []
    for _ in range(kk):
        m = jnp.max(sc, axis=0, keepdims=True)
        idx = jnp.min(jnp.where(sc == m, rows, float(nrow)), axis=0, keepdims=True)
        vals.append(m)
        idxs.append(idx)
        sc = jnp.where(rows == idx, -jnp.inf, sc)
    return jnp.concatenate(vals, axis=0), jnp.concatenate(idxs, axis=0)


def _take_rows(table, sel, kk):
    out = jnp.zeros_like(sel)
    for a in range(kk):
        out = jnp.where(sel == float(a), table[a:a + 1, :], out)
    return out


def _route_kernel(x_ref, wq_ref, keys_ref, eidx_ref, gate_ref, sc_ref, e_sc, g_sc):
    tn = x_ref.shape[0]
    xq = _mm(x_ref[...].astype(BF16), wq_ref[...])
    for hc in range(2 * PEER_HEADS):
        xq_hc = xq[:, hc * PEER_HALF:(hc + 1) * PEER_HALF].astype(BF16)
        sc_ref[hc] = _nt(keys_ref[hc], xq_hc)

    def head(h, carry):
        for half in range(tn // LANES):
            sl = slice(half * LANES, (half + 1) * LANES)
            sv1, si1 = _topk_rows(sc_ref[2 * h, :, sl], PEER_TOPK)
            sv2, si2 = _topk_rows(sc_ref[2 * h + 1, :, sl], PEER_TOPK)
            cand = jnp.concatenate([sv1[a:a + 1, :] + sv2 for a in range(PEER_TOPK)], axis=0)
            cv, ci = _topk_rows(cand, PEER_TOPK)
            a_sel = jnp.floor(ci * (1.0 / PEER_TOPK))
            b_sel = ci - a_sel * PEER_TOPK
            i1 = _take_rows(si1, a_sel, PEER_TOPK)
            i2 = _take_rows(si2, b_sel, PEER_TOPK)
            e = jnp.exp(cv - cv[0:1, :])
            e_sc[h, :, sl] = i1 * float(PEER_NKEYS) + i2
            g_sc[h, :, sl] = e / jnp.sum(e, axis=0, keepdims=True)
        return carry

    lax.fori_loop(0, PEER_HEADS, head, 0)
    nsel = PEER_HEADS * PEER_TOPK
    e_all = e_sc[...].reshape(nsel, tn)
    g_all = g_sc[...].reshape(nsel, tn)
    eidx_ref[...] = jnp.transpose(e_all * float(PEER_WORDS)).astype(I32)
    gate_ref[...] = jnp.transpose(g_all)


def _route(x1, wq, keys):
    n = x1.shape[0]
    tn = min(TN_ROUTE, n)
    nsel = PEER_HEADS * PEER_TOPK
    qw = PEER_HEADS * 2 * PEER_HALF
    return pl.pallas_call(
        _route_kernel,
        grid=(n // tn,),
        in_specs=[pl.BlockSpec((tn, D_MODEL), lambda i: (i, 0)),
                  pl.BlockSpec((D_MODEL, qw), lambda i: (0, 0)),
                  pl.BlockSpec((2 * PEER_HEADS, PEER_NKEYS, PEER_HALF), lambda i: (0, 0, 0))],
        out_specs=[pl.BlockSpec((tn, nsel), lambda i: (i, 0)),
                   pl.BlockSpec((tn, nsel), lambda i: (i, 0))],
        out_shape=[jax.ShapeDtypeStruct((n, nsel), I32),
                   jax.ShapeDtypeStruct((n, nsel), F32)],
        scratch_shapes=[pltpu.VMEM((2 * PEER_HEADS, PEER_NKEYS, tn), F32),
                        pltpu.VMEM((PEER_HEADS, PEER_TOPK, tn), F32),
                        pltpu.VMEM((PEER_HEADS, PEER_TOPK, tn), F32)],
        compiler_params=_cparams("arbitrary"),
        name="peer_route",
    )(x1, wq, keys)


def _pack_table(t):
    ne = t.shape[0]
    bits = lax.bitcast_convert_type(t.astype(BF16), jnp.uint16).astype(jnp.uint32)
    bits = bits.reshape(ne, PEER_WORDS, 2, LANES)
    words = (bits[:, :, 1, :] << 16) | bits[:, :, 0, :]
    return lax.bitcast_convert_type(words, I32).reshape(ne * PEER_WORDS, LANES)


def _stage_rows(idx_ref, tab_ref, stage, t):
    nsel = PEER_HEADS * PEER_TOPK
    for k in range(nsel):
        i = pl.multiple_of(idx_ref[0, 0, t * nsel + k], PEER_WORDS)
        stage[PEER_WORDS * k:PEER_WORDS * (k + 1), :] = tab_ref[pl.ds(i, PEER_WORDS), :]
    w = stage[...]
    lo = lax.bitcast_convert_type(w << 16, F32).astype(BF16)
    hi = lax.bitcast_convert_type(w & jnp.int32(-65536), F32).astype(BF16)
    return lo, hi


def _diag_masks():
    r = lax.broadcasted_iota(I32, (SUBLANES, PEER_STAGE), 0)
    j = lax.broadcasted_iota(I32, (SUBLANES, PEER_STAGE), 1) & (PEER_WORDS - 1)
    return r == 2 * j, r == 2 * j + 1


def _peer_u_kernel(idx_ref, tab_ref, x_ref, gate_ref, w_ref, stage, hbuf):
    tn = x_ref.shape[0]
    m_lo, m_hi = _diag_masks()

    def token(t, carry):
        lo, hi = _stage_rows(idx_ref, tab_ref, stage, t)
        x8 = x_ref[t].astype(BF16)
        part = jnp.where(m_lo, _nt(x8, lo), 0.0) + jnp.where(m_hi, _nt(x8, hi), 0.0)
        hbuf[pl.ds(t, 1), :] = jnp.sum(part, axis=0, keepdims=True)
        return carry

    lax.fori_loop(0, tn, token, 0)

    h = hbuf[...]
    n = lax.broadcasted_iota(I32, h.shape, 1)
    h = h + jnp.where((n & 1) == 0, pltpu.roll(h, PEER_STAGE - 1, 1), pltpu.roll(h, 1, 1))
    h = h + jnp.where((n & 2) == 0, pltpu.roll(h, PEER_STAGE - 2, 1), pltpu.roll(h, 2, 1))
    e_r = lax.broadcasted_iota(I32, (PEER_HEADS * PEER_TOPK, PEER_STAGE), 0)
    e_c = _div_pow2(lax.broadcasted_iota(I32, (PEER_HEADS * PEER_TOPK, PEER_STAGE), 1), PEER_WORDS)
    g4 = _mm(gate_ref[...], (e_r == e_c).astype(F32), precision=HIGHEST)
    w_ref[...] = g4 * jax.nn.gelu(h)


def _peer_v_kernel(idx_ref, tab_ref, w_ref, o_ref, stage):
    tn = w_ref.shape[0]
    m_lo, m_hi = _diag_masks()

    def token(t, carry):
        lo, hi = _stage_rows(idx_ref, tab_ref, stage, t)
        w_t = w_ref[pl.ds(t, 1), :]
        l_lo = jnp.where(m_lo, w_t, 0.0).astype(BF16)
        l_hi = jnp.where(m_hi, w_t, 0.0).astype(BF16)
        o_ref[t] = _mm(l_lo, lo) + _mm(l_hi, hi)
        return carry

    lax.fori_loop(0, tn, token, 0)


def _peer_specs(n, tn):
    nsel = PEER_HEADS * PEER_TOPK
    idx_spec = pl.BlockSpec((1, 1, tn * nsel), lambda i: (i, 0, 0), memory_space=pltpu.SMEM)
    tab_spec = pl.BlockSpec((PEER_N * PEER_WORDS, LANES), lambda i: (0, 0), pipeline_mode=pl.Buffered(1))
    return idx_spec, tab_spec


def _peer_u(eidx3, utab, x3, gates):
    n = x3.shape[0]
    tn = min(TN_PEER, n)
    nsel = PEER_HEADS * PEER_TOPK
    idx_spec, tab_spec = _peer_specs(n, tn)
    return pl.pallas_call(
        _peer_u_kernel,
        grid=(n // tn,),
        in_specs=[idx_spec, tab_spec,
                  pl.BlockSpec((tn, SUBLANES, LANES), lambda i: (i, 0, 0)),
                  pl.BlockSpec((tn, nsel), lambda i: (i, 0))],
        out_specs=pl.BlockSpec((tn, PEER_STAGE), lambda i: (i, 0)),
        out_shape=jax.ShapeDtypeStruct((n, PEER_STAGE), F32),
        scratch_shapes=[pltpu.VMEM((PEER_STAGE, LANES), I32),
                        pltpu.VMEM((tn, PEER_STAGE), F32)],
        compiler_params=_cparams("arbitrary"),
        name="peer_u",
    )(eidx3, utab, x3, gates)


def _peer_v(eidx3, vtab, w4):
    n = w4.shape[0]
    tn = min(TN_PEER, n)
    idx_spec, tab_spec = _peer_specs(n, tn)
    return pl.pallas_call(
        _peer_v_kernel,
        grid=(n // tn,),
        in_specs=[idx_spec, tab_spec,
                  pl.BlockSpec((tn, PEER_STAGE), lambda i: (i, 0))],
        out_specs=pl.BlockSpec((tn, SUBLANES, LANES), lambda i: (i, 0, 0)),
        out_shape=jax.ShapeDtypeStruct((n, SUBLANES, LANES), F32),
        scratch_shapes=[pltpu.VMEM((PEER_STAGE, LANES), I32)],
        compiler_params=_cparams("arbitrary"),
        name="peer_v",
    )(eidx3, vtab, w4)


def _ple_kernel(x_ref, peer_ref, p_ref, gw_ref, gb_ref, pw_ref, g_ref, beta_ref, o_ref):
    r = DEEPNORM_ALPHA * x_ref[...] + peer_ref[...]
    gate = _sigmoid(_mm(r.astype(BF16), gw_ref[...]) + gb_ref[...])
    r = r + gate * _mm(p_ref[...].astype(BF16), pw_ref[...])
    o_ref[...] = _layer_norm(r, g_ref[...], beta_ref[...])


def _ple(x1, peer_out, p_all, layer, gw, gb, pw, g, beta):
    n = x1.shape[0]
    tn = min(TN_PROJ, n)
    nb = n // tn
    tok = lambda w: pl.BlockSpec((tn, w), lambda i: (i, 0))
    full = lambda r, w: pl.BlockSpec((r, w), lambda i: (0, 0))
    return pl.pallas_call(
        _ple_kernel,
        grid=(nb,),
        in_specs=[tok(D_MODEL), tok(D_MODEL),
                  pl.BlockSpec((tn, PLE_DIM), lambda i: (layer * nb + i, 0)),
                  full(D_MODEL, D_MODEL), full(1, D_MODEL), full(PLE_DIM, D_MODEL),
                  full(1, D_MODEL), full(1, D_MODEL)],
        out_specs=tok(D_MODEL),
        out_shape=jax.ShapeDtypeStruct((n, D_MODEL), F32),
        compiler_params=_cparams("arbitrary"),
        name="ple_ln",
    )(x1, peer_out, p_all, gw, gb, pw, g, beta)


def _pad_cols(w, width):
    return jnp.pad(w, ((0, 0), (0, width - w.shape[1])))


def _layout_w_in(w):
    qk = GLA_HEADS * GLA_DK
    o = 0
    a = w[:, o:o + A_W]; o += A_W
    bq = w[:, o:o + qk]; o += qk
    bk = w[:, o:o + qk]; o += qk
    bv = w[:, o:o + GLA_VW]; o += GLA_VW
    bg = w[:, o:o + GLA_VW]; o += GLA_VW
    blr = w[:, o:o + GLA_GATE_RANK]; o += GLA_GATE_RANK
    c = w[:, o:o + C_W]; o += C_W
    cf = w[:, o:o + FOX_HEADS]
    wp = jnp.concatenate([a, _pad_cols(bq, GLA_DKP), _pad_cols(bk, GLA_DKP), bv, bg,
                          _pad_cols(blr, LANES), c], axis=1).astype(BF16)
    wft = jnp.pad(cf.T, ((0, SUBLANES - FOX_HEADS), (0, 0)))
    return wp, wft


def _layer(x, p_all, layer, bsz, seq, prm):
    (w_in, conv_w, conv_b, conv_ln_g, conv_ln_b, gla_gate_w, gla_gate_b, gla_norm_g, fox_forget_b,
     w_out, ln1_g, ln1_b, peer_wq, peer_keys, peer_u, peer_v, ple_w, ple_gw, ple_gb, ln2_g, ln2_b) = prm
    n = bsz * seq
    row = lambda v: v.reshape(1, -1)

    wp, wft = _layout_w_in(w_in)
    ha, hb, hc, ft = _inproj(x, wp, wft)

    a_out = _conv_module(ha, conv_w, row(conv_b), row(conv_ln_g), row(conv_ln_b), bsz, seq)

    w2p = jnp.pad(gla_gate_w, ((0, LANES - GLA_GATE_RANK), (0, GLA_DKP - GLA_HEADS * GLA_DK))).astype(BF16)
    b2p = _pad_cols(row(gla_gate_b), GLA_DKP)
    b_out = _gla(hb, w2p, b2p, row(gla_norm_g), bsz, seq)

    bfp = jnp.pad(fox_forget_b, (0, SUBLANES - FOX_HEADS)).reshape(SUBLANES, 1)
    c = _fox_cumsum(ft, bfp, bsz, seq)
    tq = min(TQ_FOX, seq)
    c_out = _fox(hc, c.reshape(SUBLANES // 2, 2, n // tq, tq), bsz, seq)

    wo = w_out.astype(BF16)
    x1 = _outproj(a_out, b_out, c_out, x, wo[0:CONV_CH], wo[CONV_CH:CONV_CH + GLA_VW],
                  wo[CONV_CH + GLA_VW:], row(ln1_g), row(ln1_b))

    keys = peer_keys.reshape(2 * PEER_HEADS, PEER_NKEYS, PEER_HALF).astype(BF16)
    eidx, gates = _route(x1, peer_wq.astype(BF16), keys)
    tn = min(TN_PEER, n)
    eidx3 = eidx.reshape(n // tn, 1, tn * PEER_HEADS * PEER_TOPK)
    w4 = _peer_u(eidx3, _pack_table(peer_u), x1.reshape(n, SUBLANES, LANES), gates)
    peer_out = _peer_v(eidx3, _pack_table(peer_v), w4).reshape(n, D_MODEL)

    return _ple(x1, peer_out, p_all, layer, ple_gw.astype(BF16), row(ple_gb), ple_w.astype(BF16),
                row(ln2_g), row(ln2_b))


def kernel(x, p, w_in, conv_w, conv_b, conv_ln_g, conv_ln_b, gla_gate_w, gla_gate_b, gla_norm_g, fox_forget_b, w_out, ln1_g, ln1_b, peer_wq, peer_keys, peer_u, peer_v, ple_w, ple_gw, ple_gb, ln2_g, ln2_b):
    bsz, seq, d = x.shape
    depth = p.shape[0]
    xf = x.reshape(bsz * seq, d)
    p_all = p.reshape(depth * bsz * seq, PLE_DIM)
    stacked = (w_in, conv_w, conv_b, conv_ln_g, conv_ln_b, gla_gate_w, gla_gate_b, gla_norm_g, fox_forget_b,
               w_out, ln1_g, ln1_b, peer_wq, peer_keys, peer_u, peer_v, ple_w, ple_gw, ple_gb, ln2_g, ln2_b)
    for i in range(depth):
        xf = _layer(xf, p_all, i, bsz, seq, tuple(t[i] for t in stacked))
    return xf.reshape(bsz, seq, d)
```

```python
import functools

import jax
import jax.numpy as jnp
from jax import lax
from jax.experimental import pallas as pl
from jax.experimental.pallas import tpu as pltpu

F32, BF16, I32 = jnp.float32, jnp.bfloat16, jnp.int32
HIGHEST = lax.Precision.HIGHEST

D_MODEL = 1024
DEPTH = 4
CHUNK = 64
PLE_DIM = 256
CONV_CH = 256
CONV_WIDTH = 31
GLA_HEADS = 6
GLA_DK = 32
GLA_DV = 64
GLA_GATE_RANK = 16
GLA_GATE_TAU = 16.0
FOX_HEADS = 6
FOX_DH = 64
PEER_HEADS = 8
PEER_NKEYS = 128
PEER_N = PEER_NKEYS * PEER_NKEYS
PEER_HALF = 128
PEER_TOPK = 16
DEEPNORM_ALPHA = (2.0 * DEPTH) ** 0.25
LN_EPS = 1e-5

LANES = 128
SUBLANES = 8
VMEM_LIMIT = 48 * 1024 * 1024

GLA_DKP = 256
GLA_VW = GLA_HEADS * GLA_DV
FOX_W = FOX_HEADS * FOX_DH
A_W = 2 * CONV_CH
B_W = 2 * GLA_DKP + 2 * GLA_VW + LANES
C_W = 3 * FOX_W
W_IN_COLS = A_W + B_W + C_W

TN_PROJ = 512
TS_CONV = 512
TQ_FOX = 256
TN_ROUTE = 256
TN_PEER = 64
PEER_WORDS = 4
PEER_STAGE = PEER_HEADS * PEER_TOPK * PEER_WORDS
PEER_UNROLL = 4


def _cparams(*sem):
    return pltpu.CompilerParams(dimension_semantics=sem, vmem_limit_bytes=VMEM_LIMIT)


def _nt(a, b, precision=None):
    return lax.dot_general(a, b, (((1,), (1,)), ((), ())),
                           preferred_element_type=F32, precision=precision)


def _mm(a, b, precision=None):
    return jnp.dot(a, b, preferred_element_type=F32, precision=precision)


def _layer_norm(r, g, b):
    mu = jnp.mean(r, axis=-1, keepdims=True)
    d = r - mu
    var = jnp.mean(d * d, axis=-1, keepdims=True)
    return d * lax.rsqrt(var + LN_EPS) * g + b


def _log_sigmoid(x):
    return jnp.minimum(x, 0.0) - jnp.log(1.0 + jnp.exp(-jnp.abs(x)))


def _sigmoid(x):
    return 1.0 / (1.0 + jnp.exp(-x))


def _div_pow2(i, d):
    assert d & (d - 1) == 0
    return lax.shift_right_logical(i, jnp.int32(d.bit_length() - 1))


def _inproj_kernel(x_ref, w_ref, wf_ref, ha_ref, hb_ref, hc_ref, ft_ref):
    x = x_ref[...]
    xb = x.astype(BF16)
    ha_ref[...] = _mm(xb, w_ref[:, 0:A_W])
    hb_ref[...] = _mm(xb, w_ref[:, A_W:A_W + B_W])
    hc_ref[...] = _mm(xb, w_ref[:, A_W + B_W:W_IN_COLS]).astype(BF16)
    ft_ref[...] = _nt(wf_ref[...], x, precision=HIGHEST)


def _inproj(x, wp, wft):
    n = x.shape[0]
    tn = min(TN_PROJ, n)
    return pl.pallas_call(
        _inproj_kernel,
        grid=(n // tn,),
        in_specs=[pl.BlockSpec((tn, D_MODEL), lambda i: (i, 0)),
                  pl.BlockSpec((D_MODEL, W_IN_COLS), lambda i: (0, 0)),
                  pl.BlockSpec((SUBLANES, D_MODEL), lambda i: (0, 0))],
        out_specs=[pl.BlockSpec((tn, A_W), lambda i: (i, 0)),
                   pl.BlockSpec((tn, B_W), lambda i: (i, 0)),
                   pl.BlockSpec((tn, C_W), lambda i: (i, 0)),
                   pl.BlockSpec((SUBLANES, tn), lambda i: (0, i))],
        out_shape=[jax.ShapeDtypeStruct((n, A_W), F32),
                   jax.ShapeDtypeStruct((n, B_W), F32),
                   jax.ShapeDtypeStruct((n, C_W), BF16),
                   jax.ShapeDtypeStruct((SUBLANES, n), F32)],
        compiler_params=_cparams("arbitrary"),
        name="inproj",
    )(x, wp, wft)


CONV_HALO = 32
CONV_SUB = 64


def _conv_kernel(ha_ref, w_ref, b_ref, g_ref, beta_ref, o_ref, ubuf):
    j = pl.program_id(1)
    ts = ha_ref.shape[0]

    @pl.when(j == 0)
    def _():
        ubuf[0:CONV_HALO, :] = jnp.zeros((CONV_HALO, CONV_CH), F32)

    @pl.when(j > 0)
    def _():
        ubuf[0:CONV_HALO, :] = ubuf[ts:ts + CONV_HALO, :]

    ubuf[CONV_HALO:CONV_HALO + ts, :] = ha_ref[:, 0:CONV_CH] * _sigmoid(ha_ref[:, CONV_CH:2 * CONV_CH])

    off = CONV_HALO - (CONV_WIDTH - 1)

    for base in range(0, ts, CONV_SUB):
        acc = jnp.zeros((CONV_SUB, CONV_CH), F32) + b_ref[...]
        for t in range(CONV_WIDTH):
            acc = acc + w_ref[t:t + 1, :] * ubuf[base + off + t:base + off + t + CONV_SUB, :]
        y = _layer_norm(acc, g_ref[...], beta_ref[...])
        o_ref[base:base + CONV_SUB, :] = (y * _sigmoid(y)).astype(BF16)


def _conv_module(ha, w, b, g, beta, bsz, seq):
    ts = min(TS_CONV, seq)
    nb = seq // ts
    return pl.pallas_call(
        _conv_kernel,
        grid=(bsz, nb),
        in_specs=[pl.BlockSpec((ts, A_W), lambda bi, j: (bi * nb + j, 0)),
                  pl.BlockSpec((CONV_WIDTH, CONV_CH), lambda bi, j: (0, 0)),
                  pl.BlockSpec((1, CONV_CH), lambda bi, j: (0, 0)),
                  pl.BlockSpec((1, CONV_CH), lambda bi, j: (0, 0)),
                  pl.BlockSpec((1, CONV_CH), lambda bi, j: (0, 0))],
        out_specs=pl.BlockSpec((ts, CONV_CH), lambda bi, j: (bi * nb + j, 0)),
        out_shape=jax.ShapeDtypeStruct((bsz * seq, CONV_CH), BF16),
        scratch_shapes=[pltpu.VMEM((CONV_HALO + ts, CONV_CH), F32)],
        compiler_params=_cparams("arbitrary", "arbitrary"),
        name="conv_module",
    )(ha, w, b, g, beta)


GLA_ROW_UNROLL = 8


def _gla_kernel(hb_ref, w2_ref, b2_ref, ng_ref, o_ref, st_ref, cum_ref, q_sc, abuf):
    L = CHUNK

    @pl.when(pl.program_id(1) == 0)
    def _():
        st_ref[...] = jnp.zeros(st_ref.shape, F32)

    q = hb_ref[:, 0:GLA_DKP] * (GLA_DK ** -0.5)
    k = hb_ref[:, GLA_DKP:2 * GLA_DKP]
    v = hb_ref[:, 2 * GLA_DKP:2 * GLA_DKP + GLA_VW]
    g_out = hb_ref[:, 2 * GLA_DKP + GLA_VW:2 * GLA_DKP + 2 * GLA_VW]
    lr = hb_ref[:, 2 * GLA_DKP + 2 * GLA_VW:B_W]

    z = _mm(lr.astype(BF16), w2_ref[...]) + b2_ref[...]
    gk = _log_sigmoid(z) * (1.0 / GLA_GATE_TAU)

    r_i = lax.broadcasted_iota(I32, (L, L), 0)
    c_i = lax.broadcasted_iota(I32, (L, L), 1)
    causal = r_i >= c_i
    cum = _mm(causal.astype(F32), gk, precision=HIGHEST)
    cum_ref[...] = cum
    q_sc[...] = q

    hm_r = lax.broadcasted_iota(I32, (SUBLANES, GLA_DKP), 0)
    hm_c = lax.broadcasted_iota(I32, (SUBLANES, GLA_DKP), 1)
    head_sel = _div_pow2(hm_c, GLA_DK) == hm_r

    vb = v.astype(BF16)

    def rows(i, carry):
        for u in range(GLA_ROW_UNROLL):
            t = i * GLA_ROW_UNROLL + u
            cum_t = cum_ref[pl.ds(t, 1), :]
            q_t = q_sc[pl.ds(t, 1), :]
            w = k * jnp.exp(jnp.minimum(cum_t - cum, 0.0))
            qm = jnp.where(head_sel, q_t, 0.0)
            abuf[pl.ds(pl.multiple_of(t * SUBLANES, SUBLANES), SUBLANES), :] = _nt(qm.astype(BF16), w.astype(BF16))
        return carry

    lax.fori_loop(0, L // GLA_ROW_UNROLL, rows, 0)

    st = st_ref[...]
    inter = _nt((q * jnp.exp(cum)).astype(BF16), st.astype(BF16))

    lane_head = _div_pow2(lax.broadcasted_iota(I32, (L, GLA_VW), 1), GLA_DV)
    o = inter
    for h in range(GLA_HEADS):
        a_h = abuf[pl.ds(h, L, stride=SUBLANES), :]
        a_h = jnp.where(causal, a_h, 0.0)
        p = _mm(a_h.astype(BF16), vb)
        o = o + jnp.where(lane_head == h, p, 0.0)

    last = cum[L - 1:L, :]
    kdec = k * jnp.exp(last - cum)
    upd = lax.dot_general(vb, kdec.astype(BF16), (((0,), (0,)), ((), ())), preferred_element_type=F32)
    s_r = _div_pow2(lax.broadcasted_iota(I32, st_ref.shape, 0), GLA_DV)
    s_c = _div_pow2(lax.broadcasted_iota(I32, st_ref.shape, 1), GLA_DK)
    st_ref[...] = st * jnp.exp(last) + jnp.where(s_r == s_c, upd, 0.0)

    o2 = o * o
    ms = jnp.zeros_like(o)
    for h in range(GLA_HEADS):
        sel = lane_head == h
        s = jnp.sum(jnp.where(sel, o2, 0.0), axis=-1, keepdims=True)
        ms = jnp.where(sel, s, ms)
    o = o * lax.rsqrt(ms * (1.0 / GLA_DV) + LN_EPS) * ng_ref[...]
    o_ref[...] = (o * (g_out * _sigmoid(g_out))).astype(BF16)


def _gla(hb, w2p, b2p, ng, bsz, seq):
    nc = seq // CHUNK
    return pl.pallas_call(
        _gla_kernel,
        grid=(bsz, nc),
        in_specs=[pl.BlockSpec((CHUNK, B_W), lambda bi, c: (bi * nc + c, 0)),
                  pl.BlockSpec((LANES, GLA_DKP), lambda bi, c: (0, 0)),
                  pl.BlockSpec((1, GLA_DKP), lambda bi, c: (0, 0)),
                  pl.BlockSpec((1, GLA_VW), lambda bi, c: (0, 0))],
        out_specs=pl.BlockSpec((CHUNK, GLA_VW), lambda bi, c: (bi * nc + c, 0)),
        out_shape=jax.ShapeDtypeStruct((bsz * seq, GLA_VW), BF16),
        scratch_shapes=[pltpu.VMEM((GLA_VW, GLA_DKP), F32),
                        pltpu.VMEM((CHUNK, GLA_DKP), F32),
                        pltpu.VMEM((CHUNK, GLA_DKP), F32),
                        pltpu.VMEM((CHUNK * SUBLANES, CHUNK), F32)],
        compiler_params=_cparams("arbitrary", "arbitrary"),
        name="gla",
    )(hb, w2p, b2p, ng)


def _fcum_kernel(ft_ref, bf_ref, c_ref):
    seq = ft_ref.shape[1]
    r_i = lax.broadcasted_iota(I32, (LANES, LANES), 0)
    c_i = lax.broadcasted_iota(I32, (LANES, LANES), 1)
    tri = (r_i <= c_i).astype(F32)
    carry = jnp.zeros((SUBLANES, 1), F32)
    for blk in range(seq // LANES):
        sl = slice(blk * LANES, (blk + 1) * LANES)
        ls = _log_sigmoid(ft_ref[:, sl] + bf_ref[...])
        cs = _mm(ls, tri, precision=HIGHEST) + carry
        c_ref[:, sl] = cs
        carry = cs[:, LANES - 1:LANES]


def _fox_cumsum(ft, bfp, bsz, seq):
    return pl.pallas_call(
        _fcum_kernel,
        grid=(bsz,),
        in_specs=[pl.BlockSpec((SUBLANES, seq), lambda bi: (0, bi)),
                  pl.BlockSpec((SUBLANES, 1), lambda bi: (0, 0))],
        out_specs=pl.BlockSpec((SUBLANES, seq), lambda bi: (0, bi)),
        out_shape=jax.ShapeDtypeStruct((SUBLANES, bsz * seq), F32),
        compiler_params=_cparams("arbitrary"),
        name="fox_cumsum",
    )(ft, bfp)


def _fox_kernel(q_ref, k_ref, v_ref, c_ref, o_ref, vt_sc, cb_sc):
    tq = q_ref.shape[0]
    nkb = v_ref.shape[0] // tq
    qi = pl.program_id(2)

    @pl.when(qi == 0)
    def _():
        def prep(kb, carry):
            k0 = pl.multiple_of(kb * tq, tq)
            vt_sc[kb] = jnp.transpose(v_ref[pl.ds(k0, tq), :].astype(F32)).astype(BF16)
            for j in range(2):
                c_row = c_ref[j, pl.ds(kb, 1), :]
                cb_sc[j * nkb + kb] = jnp.transpose(jnp.broadcast_to(c_row, (LANES, tq)))
            return carry

        lax.fori_loop(0, nkb, prep, 0)

    row = lax.broadcasted_iota(I32, (LANES, tq), 0)
    first = row < FOX_DH
    q_t = jnp.transpose(q_ref[...].astype(F32)) * (FOX_DH ** -0.5)
    q_heads = (jnp.where(first, q_t, 0.0).astype(BF16), jnp.where(first, 0.0, q_t).astype(BF16))
    c_q = tuple(c_ref[j, pl.ds(qi, 1), :] for j in range(2))
    s_i = lax.broadcasted_iota(I32, (tq, tq), 0)
    t_i = lax.broadcasted_iota(I32, (tq, tq), 1)
    causal = s_i <= t_i

    def step(kb, carry, masked):
        k0 = pl.multiple_of(kb * tq, tq)
        k_blk = k_ref[pl.ds(k0, tq), :]
        v_t = vt_sc[kb]
        out = []
        for j in range(2):
            c_k = cb_sc[j * nkb + kb]
            for g in range(tq // LANES):
                m, l, acc = carry[j * (tq // LANES) + g]
                sl = slice(g * LANES, (g + 1) * LANES)
                s = _mm(k_blk, q_heads[j][:, sl]) + (c_q[j][:, sl] - c_k)
                if masked:
                    s = jnp.where(causal[:, sl], s, -jnp.inf)
                m_new = jnp.maximum(m, jnp.max(s, axis=0, keepdims=True))
                alpha = jnp.exp(m - m_new)
                p = jnp.exp(s - m_new)
                l = alpha * l + jnp.sum(p, axis=0, keepdims=True)
                acc = alpha * acc + _mm(v_t, p.astype(BF16))
                out.append((m_new, l, acc))
        return tuple(out)

    ng = tq // LANES
    init = (jnp.full((1, LANES), -jnp.inf, F32), jnp.zeros((1, LANES), F32), jnp.zeros((LANES, LANES), F32))
    carry = lax.fori_loop(0, qi, functools.partial(step, masked=False), (init,) * (2 * ng))
    fin = step(qi, carry, True)
    head = [jnp.concatenate([fin[j * ng + g][2] / fin[j * ng + g][1] for g in range(ng)], axis=1) for j in range(2)]
    o_ref[...] = jnp.transpose(jnp.where(first, head[0], head[1])).astype(BF16)


def _fox(hc, c4, bsz, seq):
    tq = min(TQ_FOX, seq)
    nq = seq // tq
    pairs = FOX_HEADS // 2
    return pl.pallas_call(
        _fox_kernel,
        grid=(bsz, pairs, nq),
        in_specs=[pl.BlockSpec((tq, LANES), lambda bi, p, i: (bi * nq + i, p)),
                  pl.BlockSpec((seq, LANES), lambda bi, p, i: (bi, pairs + p)),
                  pl.BlockSpec((seq, LANES), lambda bi, p, i: (bi, 2 * pairs + p)),
                  pl.BlockSpec((None, 2, nq, tq), lambda bi, p, i: (p, 0, bi, 0))],
        out_specs=pl.BlockSpec((tq, LANES), lambda bi, p, i: (bi * nq + i, p)),
        out_shape=jax.ShapeDtypeStruct((bsz * seq, FOX_W), BF16),
        scratch_shapes=[pltpu.VMEM((nq, LANES, tq), BF16),
                        pltpu.VMEM((2 * nq, tq, LANES), F32)],
        compiler_params=_cparams("arbitrary", "arbitrary", "arbitrary"),
        name="fox_attention",
    )(hc, hc, hc, c4)


def _outproj_kernel(a_ref, b_ref, c_ref, x_ref, wa_ref, wb_ref, wc_ref, g_ref, beta_ref, o_ref):
    mix = _mm(a_ref[...], wa_ref[...]) + _mm(b_ref[...], wb_ref[...]) + _mm(c_ref[...], wc_ref[...])
    o_ref[...] = _layer_norm(DEEPNORM_ALPHA * x_ref[...] + mix, g_ref[...], beta_ref[...])


def _outproj(a, b, c, x, wa, wb, wc, g, beta):
    n = x.shape[0]
    tn = min(TN_PROJ, n)
    tok = lambda w: pl.BlockSpec((tn, w), lambda i: (i, 0))
    full = lambda r, w: pl.BlockSpec((r, w), lambda i: (0, 0))
    return pl.pallas_call(
        _outproj_kernel,
        grid=(n // tn,),
        in_specs=[tok(CONV_CH), tok(GLA_VW), tok(FOX_W), tok(D_MODEL),
                  full(CONV_CH, D_MODEL), full(GLA_VW, D_MODEL), full(FOX_W, D_MODEL),
                  full(1, D_MODEL), full(1, D_MODEL)],
        out_specs=tok(D_MODEL),
        out_shape=jax.ShapeDtypeStruct((n, D_MODEL), F32),
        compiler_params=_cparams("arbitrary"),
        name="outproj_ln",
    )(a, b, c, x, wa, wb, wc, g, beta)


def _topk_rows(sc, kk):
    nrow = sc.shape[0]
    rows = lax.broadcasted_iota(I32, sc.shape, 0).astype(F32)
    vals, idxs = [], []
    for _ in range(kk):
        m = jnp.max(sc, axis=0, keepdims=True)
        idx = jnp.min(jnp.where(sc == m, rows, float(nrow)), axis=0, keepdims=True)
        vals.append(m)
        idxs.append(idx)
        sc = jnp.where(rows == idx, -jnp.inf, sc)
    return jnp.concatenate(vals, axis=0), jnp.concatenate(idxs, axis=0)


def _take_rows(table, sel, kk):
    out = jnp.zeros_like(sel)
    for a in range(kk):
        out = jnp.where(sel == float(a), table[a:a + 1, :], out)
    return out


def _candidate_sums(sv1, sv2):
    half = PEER_TOPK // 2
    blocks = [sv1[0:1, :] + sv2]
    blocks += [sv1[a:a + 1, :] + sv2[0:half, :] for a in range(1, half)]
    blocks += [sv1[half:, :] + sv2[0:1, :]]
    cand = jnp.concatenate(blocks, axis=0)
    n0, n1 = PEER_TOPK, PEER_TOPK + half * (half - 1)

    def a_of(r):
        q = jnp.floor((r - n0) * (1.0 / half))
        return jnp.where(r < n0, 0.0, jnp.where(r < n1, 1.0 + q, r - (n1 - half)))

    def b_of(r):
        q = jnp.floor((r - n0) * (1.0 / half))
        return jnp.where(r < n0, r, jnp.where(r < n1, r - n0 - half * q, 0.0))

    return cand, a_of, b_of


def _route_kernel(x_ref, wq_ref, keys_ref, eidx_ref, gate_ref, sc_ref, e_sc, g_sc):
    tn = x_ref.shape[0]
    xq = _mm(x_ref[...].astype(BF16), wq_ref[...])
    for hc in range(2 * PEER_HEADS):
        xq_hc = xq[:, hc * PEER_HALF:(hc + 1) * PEER_HALF].astype(BF16)
        sc_ref[hc] = _nt(keys_ref[hc], xq_hc)

    def head(h, carry):
        for half in range(tn // LANES):
            sl = slice(half * LANES, (half + 1) * LANES)
            sv1, si1 = _topk_rows(sc_ref[2 * h, :, sl], PEER_TOPK)
            sv2, si2 = _topk_rows(sc_ref[2 * h + 1, :, sl], PEER_TOPK)
            cand, a_sel, b_sel = _candidate_sums(sv1, sv2)
            cv, ci = _topk_rows(cand, PEER_TOPK)
            a_sel, b_sel = a_sel(ci), b_sel(ci)
            i1 = _take_rows(si1, a_sel, PEER_TOPK)
            i2 = _take_rows(si2, b_sel, PEER_TOPK)
            e = jnp.exp(cv - cv[0:1, :])
            e_sc[h, :, sl] = i1 * float(PEER_NKEYS) + i2
            g_sc[h, :, sl] = e / jnp.sum(e, axis=0, keepdims=True)
        return carry

    lax.fori_loop(0, PEER_HEADS, head, 0)
    nsel = PEER_HEADS * PEER_TOPK
    e_all = e_sc[...].reshape(nsel, tn)
    g_all = g_sc[...].reshape(nsel, tn)
    eidx_ref[...] = jnp.transpose(e_all * float(PEER_WORDS)).astype(I32)
    gate_ref[...] = jnp.transpose(g_all)


def _route(x1, wq, keys):
    n = x1.shape[0]
    tn = min(TN_ROUTE, n)
    nsel = PEER_HEADS * PEER_TOPK
    qw = PEER_HEADS * 2 * PEER_HALF
    return pl.pallas_call(
        _route_kernel,
        grid=(n // tn,),
        in_specs=[pl.BlockSpec((tn, D_MODEL), lambda i: (i, 0)),
                  pl.BlockSpec((D_MODEL, qw), lambda i: (0, 0)),
                  pl.BlockSpec((2 * PEER_HEADS, PEER_NKEYS, PEER_HALF), lambda i: (0, 0, 0))],
        out_specs=[pl.BlockSpec((tn, nsel), lambda i: (i, 0)),
                   pl.BlockSpec((tn, nsel), lambda i: (i, 0))],
        out_shape=[jax.ShapeDtypeStruct((n, nsel), I32),
                   jax.ShapeDtypeStruct((n, nsel), F32)],
        scratch_shapes=[pltpu.VMEM((2 * PEER_HEADS, PEER_NKEYS, tn), F32),
                        pltpu.VMEM((PEER_HEADS, PEER_TOPK, tn), F32),
                        pltpu.VMEM((PEER_HEADS, PEER_TOPK, tn), F32)],
        compiler_params=_cparams("arbitrary"),
        name="peer_route",
    )(x1, wq, keys)


def _pack_table(t):
    ne = t.shape[0]
    bits = lax.bitcast_convert_type(t.astype(BF16), jnp.uint16).astype(jnp.uint32)
    bits = bits.reshape(ne, PEER_WORDS, 2, LANES)
    words = (bits[:, :, 1, :] << 16) | bits[:, :, 0, :]
    return lax.bitcast_convert_type(words, I32).reshape(ne * PEER_WORDS, LANES)


def _gather_rows(idx_ref, tab_ref, stage, t):
    nsel = PEER_HEADS * PEER_TOPK
    for k in range(nsel):
        i = pl.multiple_of(idx_ref[0, 0, t * nsel + k], PEER_WORDS)
        stage[PEER_WORDS * k:PEER_WORDS * (k + 1), :] = tab_ref[pl.ds(i, PEER_WORDS), :]


def _unpack_rows(stage):
    w = stage[...]
    lo = lax.bitcast_convert_type(w << 16, F32).astype(BF16)
    hi = lax.bitcast_convert_type(w & jnp.int32(-65536), F32).astype(BF16)
    return lo, hi


def _token_pipeline(idx_ref, tab_ref, stages, tn, compute):
    _gather_rows(idx_ref, tab_ref, stages[0], 0)

    def group(i, carry):
        t0 = PEER_UNROLL * i
        for u in range(PEER_UNROLL):
            _gather_rows(idx_ref, tab_ref, stages[(u + 1) % 2], jnp.minimum(t0 + u + 1, tn - 1))
            compute(stages[u % 2], t0 + u)
        return carry

    lax.fori_loop(0, tn // PEER_UNROLL, group, 0)


def _diag_masks():
    r = lax.broadcasted_iota(I32, (SUBLANES, PEER_STAGE), 0)
    j = lax.broadcasted_iota(I32, (SUBLANES, PEER_STAGE), 1) & (PEER_WORDS - 1)
    return r == 2 * j, r == 2 * j + 1


def _peer_u_kernel(idx_ref, tab_ref, x_ref, gate_ref, w_ref, stage_a, stage_b, hbuf):
    tn = x_ref.shape[0]
    m_lo, m_hi = _diag_masks()

    def compute(stage, t):
        lo, hi = _unpack_rows(stage)
        x8 = x_ref[t].astype(BF16)
        part = jnp.where(m_lo, _nt(x8, lo), 0.0) + jnp.where(m_hi, _nt(x8, hi), 0.0)
        hbuf[pl.ds(t, 1), :] = jnp.sum(part, axis=0, keepdims=True)

    _token_pipeline(idx_ref, tab_ref, (stage_a, stage_b), tn, compute)

    h = hbuf[...]
    n = lax.broadcasted_iota(I32, h.shape, 1)
    h = h + jnp.where((n & 1) == 0, pltpu.roll(h, PEER_STAGE - 1, 1), pltpu.roll(h, 1, 1))
    h = h + jnp.where((n & 2) == 0, pltpu.roll(h, PEER_STAGE - 2, 1), pltpu.roll(h, 2, 1))
    e_r = lax.broadcasted_iota(I32, (PEER_HEADS * PEER_TOPK, PEER_STAGE), 0)
    e_c = _div_pow2(lax.broadcasted_iota(I32, (PEER_HEADS * PEER_TOPK, PEER_STAGE), 1), PEER_WORDS)
    g4 = _mm(gate_ref[...], (e_r == e_c).astype(F32), precision=HIGHEST)
    w_ref[...] = g4 * jax.nn.gelu(h)


def _peer_v_kernel(idx_ref, tab_ref, w_ref, o_ref, stage_a, stage_b):
    tn = w_ref.shape[0]
    m_lo, m_hi = _diag_masks()

    def compute(stage, t):
        lo, hi = _unpack_rows(stage)
        w_t = w_ref[pl.ds(t, 1), :]
        l_lo = jnp.where(m_lo, w_t, 0.0).astype(BF16)
        l_hi = jnp.where(m_hi, w_t, 0.0).astype(BF16)
        o_ref[t] = _mm(l_lo, lo) + _mm(l_hi, hi)

    _token_pipeline(idx_ref, tab_ref, (stage_a, stage_b), tn, compute)


def _peer_specs(n, tn):
    nsel = PEER_HEADS * PEER_TOPK
    idx_spec = pl.BlockSpec((1, 1, tn * nsel), lambda i: (i, 0, 0), memory_space=pltpu.SMEM)
    tab_spec = pl.BlockSpec((PEER_N * PEER_WORDS, LANES), lambda i: (0, 0), pipeline_mode=pl.Buffered(1))
    return idx_spec, tab_spec


def _peer_u(eidx3, utab, x3, gates):
    n = x3.shape[0]
    tn = min(TN_PEER, n)
    nsel = PEER_HEADS * PEER_TOPK
    idx_spec, tab_spec = _peer_specs(n, tn)
    return pl.pallas_call(
        _peer_u_kernel,
        grid=(n // tn,),
        in_specs=[idx_spec, tab_spec,
                  pl.BlockSpec((tn, SUBLANES, LANES), lambda i: (i, 0, 0)),
                  pl.BlockSpec((tn, nsel), lambda i: (i, 0))],
        out_specs=pl.BlockSpec((tn, PEER_STAGE), lambda i: (i, 0)),
        out_shape=jax.ShapeDtypeStruct((n, PEER_STAGE), F32),
        scratch_shapes=[pltpu.VMEM((PEER_STAGE, LANES), I32),
                        pltpu.VMEM((PEER_STAGE, LANES), I32),
                        pltpu.VMEM((tn, PEER_STAGE), F32)],
        compiler_params=_cparams("arbitrary"),
        name="peer_u",
    )(eidx3, utab, x3, gates)


def _peer_v(eidx3, vtab, w4):
    n = w4.shape[0]
    tn = min(TN_PEER, n)
    idx_spec, tab_spec = _peer_specs(n, tn)
    return pl.pallas_call(
        _peer_v_kernel,
        grid=(n // tn,),
        in_specs=[idx_spec, tab_spec,
                  pl.BlockSpec((tn, PEER_STAGE), lambda i: (i, 0))],
        out_specs=pl.BlockSpec((tn, SUBLANES, LANES), lambda i: (i, 0, 0)),
        out_shape=jax.ShapeDtypeStruct((n, SUBLANES, LANES), F32),
        scratch_shapes=[pltpu.VMEM((PEER_STAGE, LANES), I32),
                        pltpu.VMEM((PEER_STAGE, LANES), I32)],
        compiler_params=_cparams("arbitrary"),
        name="peer_v",
    )(eidx3, vtab, w4)


def _ple_kernel(x_ref, peer_ref, p_ref, gw_ref, gb_ref, pw_ref, g_ref, beta_ref, o_ref):
    r = DEEPNORM_ALPHA * x_ref[...] + peer_ref[...]
    gate = _sigmoid(_mm(r.astype(BF16), gw_ref[...]) + gb_ref[...])
    r = r + gate * _mm(p_ref[...].astype(BF16), pw_ref[...])
    o_ref[...] = _layer_norm(r, g_ref[...], beta_ref[...])


def _ple(x1, peer_out, p_all, layer, gw, gb, pw, g, beta):
    n = x1.shape[0]
    tn = min(TN_PROJ, n)
    nb = n // tn
    tok = lambda w: pl.BlockSpec((tn, w), lambda i: (i, 0))
    full = lambda r, w: pl.BlockSpec((r, w), lambda i: (0, 0))
    return pl.pallas_call(
        _ple_kernel,
        grid=(nb,),
        in_specs=[tok(D_MODEL), tok(D_MODEL),
                  pl.BlockSpec((tn, PLE_DIM), lambda i: (layer * nb + i, 0)),
                  full(D_MODEL, D_MODEL), full(1, D_MODEL), full(PLE_DIM, D_MODEL),
                  full(1, D_MODEL), full(1, D_MODEL)],
        out_specs=tok(D_MODEL),
        out_shape=jax.ShapeDtypeStruct((n, D_MODEL), F32),
        compiler_params=_cparams("arbitrary"),
        name="ple_ln",
    )(x1, peer_out, p_all, gw, gb, pw, g, beta)


def _pad_cols(w, width):
    return jnp.pad(w, ((0, 0), (0, width - w.shape[1])))


def _layout_w_in(w):
    qk = GLA_HEADS * GLA_DK
    o = 0
    a = w[:, o:o + A_W]; o += A_W
    bq = w[:, o:o + qk]; o += qk
    bk = w[:, o:o + qk]; o += qk
    bv = w[:, o:o + GLA_VW]; o += GLA_VW
    bg = w[:, o:o + GLA_VW]; o += GLA_VW
    blr = w[:, o:o + GLA_GATE_RANK]; o += GLA_GATE_RANK
    c = w[:, o:o + C_W]; o += C_W
    cf = w[:, o:o + FOX_HEADS]
    wp = jnp.concatenate([a, _pad_cols(bq, GLA_DKP), _pad_cols(bk, GLA_DKP), bv, bg,
                          _pad_cols(blr, LANES), c], axis=1).astype(BF16)
    wft = jnp.pad(cf.T, ((0, SUBLANES - FOX_HEADS), (0, 0)))
    return wp, wft


def _layer(x, p_all, layer, bsz, seq, prm):
    (w_in, conv_w, conv_b, conv_ln_g, conv_ln_b, gla_gate_w, gla_gate_b, gla_norm_g, fox_forget_b,
     w_out, ln1_g, ln1_b, peer_wq, peer_keys, peer_u, peer_v, ple_w, ple_gw, ple_gb, ln2_g, ln2_b) = prm
    n = bsz * seq
    row = lambda v: v.reshape(1, -1)

    wp, wft = _layout_w_in(w_in)
    ha, hb, hc, ft = _inproj(x, wp, wft)

    a_out = _conv_module(ha, conv_w, row(conv_b), row(conv_ln_g), row(conv_ln_b), bsz, seq)

    w2p = jnp.pad(gla_gate_w, ((0, LANES - GLA_GATE_RANK), (0, GLA_DKP - GLA_HEADS * GLA_DK))).astype(BF16)
    b2p = _pad_cols(row(gla_gate_b), GLA_DKP)
    b_out = _gla(hb, w2p, b2p, row(gla_norm_g), bsz, seq)

    bfp = jnp.pad(fox_forget_b, (0, SUBLANES - FOX_HEADS)).reshape(SUBLANES, 1)
    c = _fox_cumsum(ft, bfp, bsz, seq)
    tq = min(TQ_FOX, seq)
    c_out = _fox(hc, c.reshape(SUBLANES // 2, 2, n // tq, tq), bsz, seq)

    wo = w_out.astype(BF16)
    x1 = _outproj(a_out, b_out, c_out, x, wo[0:CONV_CH], wo[CONV_CH:CONV_CH + GLA_VW],
                  wo[CONV_CH + GLA_VW:], row(ln1_g), row(ln1_b))

    keys = peer_keys.reshape(2 * PEER_HEADS, PEER_NKEYS, PEER_HALF).astype(BF16)
    eidx, gates = _route(x1, peer_wq.astype(BF16), keys)
    tn = min(TN_PEER, n)
    eidx3 = eidx.reshape(n // tn, 1, tn * PEER_HEADS * PEER_TOPK)
    w4 = _peer_u(eidx3, _pack_table(peer_u), x1.reshape(n, SUBLANES, LANES), gates)
    peer_out = _peer_v(eidx3, _pack_table(peer_v), w4).reshape(n, D_MODEL)

    return _ple(x1, peer_out, p_all, layer, ple_gw.astype(BF16), row(ple_gb), ple_w.astype(BF16),
                row(ln2_g), row(ln2_b))


def kernel(x, p, w_in, conv_w, conv_b, conv_ln_g, conv_ln_b, gla_gate_w, gla_gate_b, gla_norm_g, fox_forget_b, w_out, ln1_g, ln1_b, peer_wq, peer_keys, peer_u, peer_v, ple_w, ple_gw, ple_gb, ln2_g, ln2_b):
    bsz, seq, d = x.shape
    depth = p.shape[0]
    xf = x.reshape(bsz * seq, d)
    p_all = p.reshape(depth * bsz * seq, PLE_DIM)
    stacked = (w_in, conv_w, conv_b, conv_ln_g, conv_ln_b, gla_gate_w, gla_gate_b, gla_norm_g, fox_forget_b,
               w_out, ln1_g, ln1_b, peer_wq, peer_keys, peer_u, peer_v, ple_w, ple_gw, ple_gb, ln2_g, ln2_b)
    for i in range(depth):
        xf = _layer(xf, p_all, i, bsz, seq, tuple(t[i] for t in stacked))
    return xf.reshape(bsz, seq, d)
```

```python
import functools

import jax
import jax.numpy as jnp
from jax import lax
from jax.experimental import pallas as pl
from jax.experimental.pallas import tpu as pltpu

F32, BF16, I32 = jnp.float32, jnp.bfloat16, jnp.int32
HIGHEST = lax.Precision.HIGHEST

D_MODEL = 1024
DEPTH = 4
CHUNK = 64
PLE_DIM = 256
CONV_CH = 256
CONV_WIDTH = 31
GLA_HEADS = 6
GLA_DK = 32
GLA_DV = 64
GLA_GATE_RANK = 16
GLA_GATE_TAU = 16.0
FOX_HEADS = 6
FOX_DH = 64
PEER_HEADS = 8
PEER_NKEYS = 128
PEER_N = PEER_NKEYS * PEER_NKEYS
PEER_HALF = 128
PEER_TOPK = 16
DEEPNORM_ALPHA = (2.0 * DEPTH) ** 0.25
LN_EPS = 1e-5

LANES = 128
SUBLANES = 8
VMEM_LIMIT = 48 * 1024 * 1024

GLA_DKP = 256
GLA_VW = GLA_HEADS * GLA_DV
FOX_W = FOX_HEADS * FOX_DH
A_W = 2 * CONV_CH
B_W = 2 * GLA_DKP + 2 * GLA_VW + LANES
C_W = 3 * FOX_W
W_IN_COLS = A_W + B_W + C_W

TN_PROJ = 512
TS_CONV = 512
TQ_FOX = 256
TN_ROUTE = 256
TN_PEER = 64
PEER_WORDS = 4
PEER_STAGE = PEER_HEADS * PEER_TOPK * PEER_WORDS
PEER_UNROLL = 16


def _cparams(*sem):
    return pltpu.CompilerParams(dimension_semantics=sem, vmem_limit_bytes=VMEM_LIMIT)


def _nt(a, b, precision=None):
    return lax.dot_general(a, b, (((1,), (1,)), ((), ())),
                           preferred_element_type=F32, precision=precision)


def _mm(a, b, precision=None):
    return jnp.dot(a, b, preferred_element_type=F32, precision=precision)


def _layer_norm(r, g, b):
    mu = jnp.mean(r, axis=-1, keepdims=True)
    d = r - mu
    var = jnp.mean(d * d, axis=-1, keepdims=True)
    return d * lax.rsqrt(var + LN_EPS) * g + b


def _log_sigmoid(x):
    return jnp.minimum(x, 0.0) - jnp.log(1.0 + jnp.exp(-jnp.abs(x)))


def _sigmoid(x):
    return 1.0 / (1.0 + jnp.exp(-x))


def _div_pow2(i, d):
    assert d & (d - 1) == 0
    return lax.shift_right_logical(i, jnp.int32(d.bit_length() - 1))


def _inproj_kernel(x_ref, w_ref, wf_ref, ha_ref, hb_ref, hc_ref, ft_ref):
    x = x_ref[...]
    xb = x.astype(BF16)
    ha_ref[...] = _mm(xb, w_ref[:, 0:A_W])
    hb_ref[...] = _mm(xb, w_ref[:, A_W:A_W + B_W])
    hc_ref[...] = _mm(xb, w_ref[:, A_W + B_W:W_IN_COLS]).astype(BF16)
    ft_ref[...] = _nt(wf_ref[...], x, precision=HIGHEST)


def _inproj(x, wp, wft):
    n = x.shape[0]
    tn = min(TN_PROJ, n)
    return pl.pallas_call(
        _inproj_kernel,
        grid=(n // tn,),
        in_specs=[pl.BlockSpec((tn, D_MODEL), lambda i: (i, 0)),
                  pl.BlockSpec((D_MODEL, W_IN_COLS), lambda i: (0, 0)),
                  pl.BlockSpec((SUBLANES, D_MODEL), lambda i: (0, 0))],
        out_specs=[pl.BlockSpec((tn, A_W), lambda i: (i, 0)),
                   pl.BlockSpec((tn, B_W), lambda i: (i, 0)),
                   pl.BlockSpec((tn, C_W), lambda i: (i, 0)),
                   pl.BlockSpec((SUBLANES, tn), lambda i: (0, i))],
        out_shape=[jax.ShapeDtypeStruct((n, A_W), F32),
                   jax.ShapeDtypeStruct((n, B_W), F32),
                   jax.ShapeDtypeStruct((n, C_W), BF16),
                   jax.ShapeDtypeStruct((SUBLANES, n), F32)],
        compiler_params=_cparams("arbitrary"),
        name="inproj",
    )(x, wp, wft)


CONV_HALO = 32
CONV_SUB = 64


def _conv_kernel(ha_ref, w_ref, b_ref, g_ref, beta_ref, o_ref, ubuf):
    j = pl.program_id(1)
    ts = ha_ref.shape[0]

    @pl.when(j == 0)
    def _():
        ubuf[0:CONV_HALO, :] = jnp.zeros((CONV_HALO, CONV_CH), F32)

    @pl.when(j > 0)
    def _():
        ubuf[0:CONV_HALO, :] = ubuf[ts:ts + CONV_HALO, :]

    ubuf[CONV_HALO:CONV_HALO + ts, :] = ha_ref[:, 0:CONV_CH] * _sigmoid(ha_ref[:, CONV_CH:2 * CONV_CH])

    off = CONV_HALO - (CONV_WIDTH - 1)

    for base in range(0, ts, CONV_SUB):
        acc = jnp.zeros((CONV_SUB, CONV_CH), F32) + b_ref[...]
        for t in range(CONV_WIDTH):
            acc = acc + w_ref[t:t + 1, :] * ubuf[base + off + t:base + off + t + CONV_SUB, :]
        y = _layer_norm(acc, g_ref[...], beta_ref[...])
        o_ref[base:base + CONV_SUB, :] = (y * _sigmoid(y)).astype(BF16)


def _conv_module(ha, w, b, g, beta, bsz, seq):
    ts = min(TS_CONV, seq)
    nb = seq // ts
    return pl.pallas_call(
        _conv_kernel,
        grid=(bsz, nb),
        in_specs=[pl.BlockSpec((ts, A_W), lambda bi, j: (bi * nb + j, 0)),
                  pl.BlockSpec((CONV_WIDTH, CONV_CH), lambda bi, j: (0, 0)),
                  pl.BlockSpec((1, CONV_CH), lambda bi, j: (0, 0)),
                  pl.BlockSpec((1, CONV_CH), lambda bi, j: (0, 0)),
                  pl.BlockSpec((1, CONV_CH), lambda bi, j: (0, 0))],
        out_specs=pl.BlockSpec((ts, CONV_CH), lambda bi, j: (bi * nb + j, 0)),
        out_shape=jax.ShapeDtypeStruct((bsz * seq, CONV_CH), BF16),
        scratch_shapes=[pltpu.VMEM((CONV_HALO + ts, CONV_CH), F32)],
        compiler_params=_cparams("arbitrary", "arbitrary"),
        name="conv_module",
    )(ha, w, b, g, beta)


GLA_ROW_UNROLL = 16


def _gla_kernel(hb_ref, w2_ref, b2_ref, ng_ref, o_ref, st_ref, cum_ref, q_sc, abuf):
    L = CHUNK

    @pl.when(pl.program_id(1) == 0)
    def _():
        st_ref[...] = jnp.zeros(st_ref.shape, F32)

    q = hb_ref[:, 0:GLA_DKP] * (GLA_DK ** -0.5)
    k = hb_ref[:, GLA_DKP:2 * GLA_DKP]
    v = hb_ref[:, 2 * GLA_DKP:2 * GLA_DKP + GLA_VW]
    g_out = hb_ref[:, 2 * GLA_DKP + GLA_VW:2 * GLA_DKP + 2 * GLA_VW]
    lr = hb_ref[:, 2 * GLA_DKP + 2 * GLA_VW:B_W]

    z = _mm(lr.astype(BF16), w2_ref[...]) + b2_ref[...]
    gk = _log_sigmoid(z) * (1.0 / GLA_GATE_TAU)

    r_i = lax.broadcasted_iota(I32, (L, L), 0)
    c_i = lax.broadcasted_iota(I32, (L, L), 1)
    causal = r_i >= c_i
    cum = _mm(causal.astype(F32), gk, precision=HIGHEST)
    cum_ref[...] = cum
    q_sc[...] = q

    hm_r = lax.broadcasted_iota(I32, (SUBLANES, GLA_DKP), 0)
    hm_c = lax.broadcasted_iota(I32, (SUBLANES, GLA_DKP), 1)
    head_sel = _div_pow2(hm_c, GLA_DK) == hm_r

    vb = v.astype(BF16)

    def rows(i, carry):
        for u in range(GLA_ROW_UNROLL):
            t = i * GLA_ROW_UNROLL + u
            cum_t = cum_ref[pl.ds(t, 1), :]
            q_t = q_sc[pl.ds(t, 1), :]
            w = k * jnp.exp(jnp.minimum(cum_t - cum, 0.0))
            qm = jnp.where(head_sel, q_t, 0.0)
            abuf[pl.ds(pl.multiple_of(t * SUBLANES, SUBLANES), SUBLANES), :] = _nt(qm.astype(BF16), w.astype(BF16))
        return carry

    lax.fori_loop(0, L // GLA_ROW_UNROLL, rows, 0)

    st = st_ref[...]
    inter = _nt((q * jnp.exp(cum)).astype(BF16), st.astype(BF16))

    lane_head = _div_pow2(lax.broadcasted_iota(I32, (L, GLA_VW), 1), GLA_DV)
    o = inter
    for h in range(GLA_HEADS):
        a_h = abuf[pl.ds(h, L, stride=SUBLANES), :]
        a_h = jnp.where(causal, a_h, 0.0)
        p = _mm(a_h.astype(BF16), vb)
        o = o + jnp.where(lane_head == h, p, 0.0)

    last = cum[L - 1:L, :]
    kdec = k * jnp.exp(last - cum)
    upd = lax.dot_general(vb, kdec.astype(BF16), (((0,), (0,)), ((), ())), preferred_element_type=F32)
    s_r = _div_pow2(lax.broadcasted_iota(I32, st_ref.shape, 0), GLA_DV)
    s_c = _div_pow2(lax.broadcasted_iota(I32, st_ref.shape, 1), GLA_DK)
    st_ref[...] = st * jnp.exp(last) + jnp.where(s_r == s_c, upd, 0.0)

    o2 = o * o
    ms = jnp.zeros_like(o)
    for h in range(GLA_HEADS):
        sel = lane_head == h
        s = jnp.sum(jnp.where(sel, o2, 0.0), axis=-1, keepdims=True)
        ms = jnp.where(sel, s, ms)
    o = o * lax.rsqrt(ms * (1.0 / GLA_DV) + LN_EPS) * ng_ref[...]
    o_ref[...] = (o * (g_out * _sigmoid(g_out))).astype(BF16)


def _gla(hb, w2p, b2p, ng, bsz, seq):
    nc = seq // CHUNK
    return pl.pallas_call(
        _gla_kernel,
        grid=(bsz, nc),
        in_specs=[pl.BlockSpec((CHUNK, B_W), lambda bi, c: (bi * nc + c, 0)),
                  pl.BlockSpec((LANES, GLA_DKP), lambda bi, c: (0, 0)),
                  pl.BlockSpec((1, GLA_DKP), lambda bi, c: (0, 0)),
                  pl.BlockSpec((1, GLA_VW), lambda bi, c: (0, 0))],
        out_specs=pl.BlockSpec((CHUNK, GLA_VW), lambda bi, c: (bi * nc + c, 0)),
        out_shape=jax.ShapeDtypeStruct((bsz * seq, GLA_VW), BF16),
        scratch_shapes=[pltpu.VMEM((GLA_VW, GLA_DKP), F32),
                        pltpu.VMEM((CHUNK, GLA_DKP), F32),
                        pltpu.VMEM((CHUNK, GLA_DKP), F32),
                        pltpu.VMEM((CHUNK * SUBLANES, CHUNK), F32)],
        compiler_params=_cparams("arbitrary", "arbitrary"),
        name="gla",
    )(hb, w2p, b2p, ng)


def _fcum_kernel(ft_ref, bf_ref, c_ref):
    seq = ft_ref.shape[1]
    r_i = lax.broadcasted_iota(I32, (LANES, LANES), 0)
    c_i = lax.broadcasted_iota(I32, (LANES, LANES), 1)
    tri = (r_i <= c_i).astype(F32)
    carry = jnp.zeros((SUBLANES, 1), F32)
    for blk in range(seq // LANES):
        sl = slice(blk * LANES, (blk + 1) * LANES)
        ls = _log_sigmoid(ft_ref[:, sl] + bf_ref[...])
        cs = _mm(ls, tri, precision=HIGHEST) + carry
        c_ref[:, sl] = cs
        carry = cs[:, LANES - 1:LANES]


def _fox_cumsum(ft, bfp, bsz, seq):
    return pl.pallas_call(
        _fcum_kernel,
        grid=(bsz,),
        in_specs=[pl.BlockSpec((SUBLANES, seq), lambda bi: (0, bi)),
                  pl.BlockSpec((SUBLANES, 1), lambda bi: (0, 0))],
        out_specs=pl.BlockSpec((SUBLANES, seq), lambda bi: (0, bi)),
        out_shape=jax.ShapeDtypeStruct((SUBLANES, bsz * seq), F32),
        compiler_params=_cparams("arbitrary"),
        name="fox_cumsum",
    )(ft, bfp)


def _fox_kernel(q_ref, k_ref, v_ref, c_ref, o_ref, vt_sc, cb_sc):
    tq = q_ref.shape[0]
    nkb = v_ref.shape[0] // tq
    qi = pl.program_id(2)

    @pl.when(qi == 0)
    def _():
        def prep(kb, carry):
            k0 = pl.multiple_of(kb * tq, tq)
            vt_sc[kb] = jnp.transpose(v_ref[pl.ds(k0, tq), :].astype(F32)).astype(BF16)
            for j in range(2):
                c_row = c_ref[j, pl.ds(kb, 1), :]
                cb_sc[j * nkb + kb] = jnp.transpose(jnp.broadcast_to(c_row, (LANES, tq)))
            return carry

        lax.fori_loop(0, nkb, prep, 0)

    row = lax.broadcasted_iota(I32, (LANES, tq), 0)
    first = row < FOX_DH
    q_t = jnp.transpose(q_ref[...].astype(F32)) * (FOX_DH ** -0.5)
    q_heads = (jnp.where(first, q_t, 0.0).astype(BF16), jnp.where(first, 0.0, q_t).astype(BF16))
    c_q = tuple(c_ref[j, pl.ds(qi, 1), :] for j in range(2))
    s_i = lax.broadcasted_iota(I32, (tq, tq), 0)
    t_i = lax.broadcasted_iota(I32, (tq, tq), 1)
    causal = s_i <= t_i

    ng = tq // LANES
    chains = [(j, g) for j in range(2) for g in range(ng)]

    def scores(kb):
        k_blk = k_ref[pl.ds(pl.multiple_of(kb * tq, tq), tq), :]
        return tuple(_mm(k_blk, q_heads[j][:, g * LANES:(g + 1) * LANES]) for j, g in chains)

    def update(kb, qk, state, masked):
        v_t = vt_sc[kb]
        out = []
        for (j, g), s, (m, l, acc) in zip(chains, qk, state):
            sl = slice(g * LANES, (g + 1) * LANES)
            s = s + (c_q[j][:, sl] - cb_sc[j * nkb + kb])
            if masked:
                s = jnp.where(causal[:, sl], s, -jnp.inf)
            m_new = jnp.maximum(m, jnp.max(s, axis=0, keepdims=True))
            alpha = jnp.exp(m - m_new)
            p = jnp.exp(s - m_new)
            l = alpha * l + jnp.sum(p, axis=0, keepdims=True)
            acc = alpha * acc + _mm(v_t, p.astype(BF16))
            out.append((m_new, l, acc))
        return tuple(out)

    def step(kb, carry):
        qk, state = carry
        nxt = scores(kb + 1)
        return nxt, update(kb, qk, state, False)

    init = (jnp.full((1, LANES), -jnp.inf, F32), jnp.zeros((1, LANES), F32), jnp.zeros((LANES, LANES), F32))
    qk, state = lax.fori_loop(0, qi, step, (scores(0), (init,) * len(chains)))
    fin = update(qi, qk, state, True)
    head = [jnp.concatenate([fin[j * ng + g][2] / fin[j * ng + g][1] for g in range(ng)], axis=1) for j in range(2)]
    o_ref[...] = jnp.transpose(jnp.where(first, head[0], head[1])).astype(BF16)


def _fox(hc, c4, bsz, seq):
    tq = min(TQ_FOX, seq)
    nq = seq // tq
    pairs = FOX_HEADS // 2
    return pl.pallas_call(
        _fox_kernel,
        grid=(bsz, pairs, nq),
        in_specs=[pl.BlockSpec((tq, LANES), lambda bi, p, i: (bi * nq + i, p)),
                  pl.BlockSpec((seq, LANES), lambda bi, p, i: (bi, pairs + p)),
                  pl.BlockSpec((seq, LANES), lambda bi, p, i: (bi, 2 * pairs + p)),
                  pl.BlockSpec((None, 2, nq, tq), lambda bi, p, i: (p, 0, bi, 0))],
        out_specs=pl.BlockSpec((tq, LANES), lambda bi, p, i: (bi * nq + i, p)),
        out_shape=jax.ShapeDtypeStruct((bsz * seq, FOX_W), BF16),
        scratch_shapes=[pltpu.VMEM((nq, LANES, tq), BF16),
                        pltpu.VMEM((2 * nq, tq, LANES), F32)],
        compiler_params=_cparams("arbitrary", "arbitrary", "arbitrary"),
        name="fox_attention",
    )(hc, hc, hc, c4)


def _outproj_kernel(a_ref, b_ref, c_ref, x_ref, wa_ref, wb_ref, wc_ref, g_ref, beta_ref, o_ref):
    mix = _mm(a_ref[...], wa_ref[...]) + _mm(b_ref[...], wb_ref[...]) + _mm(c_ref[...], wc_ref[...])
    o_ref[...] = _layer_norm(DEEPNORM_ALPHA * x_ref[...] + mix, g_ref[...], beta_ref[...])


def _outproj(a, b, c, x, wa, wb, wc, g, beta):
    n = x.shape[0]
    tn = min(TN_PROJ, n)
    tok = lambda w: pl.BlockSpec((tn, w), lambda i: (i, 0))
    full = lambda r, w: pl.BlockSpec((r, w), lambda i: (0, 0))
    return pl.pallas_call(
        _outproj_kernel,
        grid=(n // tn,),
        in_specs=[tok(CONV_CH), tok(GLA_VW), tok(FOX_W), tok(D_MODEL),
                  full(CONV_CH, D_MODEL), full(GLA_VW, D_MODEL), full(FOX_W, D_MODEL),
                  full(1, D_MODEL), full(1, D_MODEL)],
        out_specs=tok(D_MODEL),
        out_shape=jax.ShapeDtypeStruct((n, D_MODEL), F32),
        compiler_params=_cparams("arbitrary"),
        name="outproj_ln",
    )(a, b, c, x, wa, wb, wc, g, beta)


def _topk_rows(sc, kk):
    nrow = sc.shape[0]
    rows = lax.broadcasted_iota(I32, sc.shape, 0).astype(F32)
    vals, idxs = [], []
    for _ in range(kk):
        m = jnp.max(sc, axis=0, keepdims=True)
        idx = jnp.min(jnp.where(sc == m, rows, float(nrow)), axis=0, keepdims=True)
        vals.append(m)
        idxs.append(idx)
        sc = jnp.where(rows == idx, -jnp.inf, sc)
    return jnp.concatenate(vals, axis=0), jnp.concatenate(idxs, axis=0)


def _take_rows(table, sel, kk):
    out = jnp.zeros_like(sel)
    for a in range(kk):
        out = jnp.where(sel == float(a), table[a:a + 1, :], out)
    return out


def _candidate_sums(sv1, sv2):
    half = PEER_TOPK // 2
    blocks = [sv1[0:1, :] + sv2]
    blocks += [sv1[a:a + 1, :] + sv2[0:half, :] for a in range(1, half)]
    blocks += [sv1[half:, :] + sv2[0:1, :]]
    cand = jnp.concatenate(blocks, axis=0)
    n0, n1 = PEER_TOPK, PEER_TOPK + half * (half - 1)

    def a_of(r):
        q = jnp.floor((r - n0) * (1.0 / half))
        return jnp.where(r < n0, 0.0, jnp.where(r < n1, 1.0 + q, r - (n1 - half)))

    def b_of(r):
        q = jnp.floor((r - n0) * (1.0 / half))
        return jnp.where(r < n0, r, jnp.where(r < n1, r - n0 - half * q, 0.0))

    return cand, a_of, b_of


def _route_kernel(x_ref, wq_ref, keys_ref, eidx_ref, gate_ref, sc_ref, e_sc, g_sc):
    tn = x_ref.shape[0]
    xq = _mm(x_ref[...].astype(BF16), wq_ref[...])
    for hc in range(2 * PEER_HEADS):
        xq_hc = xq[:, hc * PEER_HALF:(hc + 1) * PEER_HALF].astype(BF16)
        sc_ref[hc] = _nt(keys_ref[hc], xq_hc)

    def head(h, carry):
        for half in range(tn // LANES):
            sl = slice(half * LANES, (half + 1) * LANES)
            sv1, si1 = _topk_rows(sc_ref[2 * h, :, sl], PEER_TOPK)
            sv2, si2 = _topk_rows(sc_ref[2 * h + 1, :, sl], PEER_TOPK)
            cand, a_sel, b_sel = _candidate_sums(sv1, sv2)
            cv, ci = _topk_rows(cand, PEER_TOPK)
            a_sel, b_sel = a_sel(ci), b_sel(ci)
            i1 = _take_rows(si1, a_sel, PEER_TOPK)
            i2 = _take_rows(si2, b_sel, PEER_TOPK)
            e = jnp.exp(cv - cv[0:1, :])
            e_sc[h, :, sl] = i1 * float(PEER_NKEYS) + i2
            g_sc[h, :, sl] = e / jnp.sum(e, axis=0, keepdims=True)
        return carry

    lax.fori_loop(0, PEER_HEADS, head, 0)
    nsel = PEER_HEADS * PEER_TOPK
    e_all = e_sc[...].reshape(nsel, tn)
    g_all = g_sc[...].reshape(nsel, tn)
    eidx_ref[...] = jnp.transpose(e_all * float(PEER_WORDS)).astype(I32)
    gate_ref[...] = jnp.transpose(g_all)


def _route(x1, wq, keys):
    n = x1.shape[0]
    tn = min(TN_ROUTE, n)
    nsel = PEER_HEADS * PEER_TOPK
    qw = PEER_HEADS * 2 * PEER_HALF
    return pl.pallas_call(
        _route_kernel,
        grid=(n // tn,),
        in_specs=[pl.BlockSpec((tn, D_MODEL), lambda i: (i, 0)),
                  pl.BlockSpec((D_MODEL, qw), lambda i: (0, 0)),
                  pl.BlockSpec((2 * PEER_HEADS, PEER_NKEYS, PEER_HALF), lambda i: (0, 0, 0))],
        out_specs=[pl.BlockSpec((tn, nsel), lambda i: (i, 0)),
                   pl.BlockSpec((tn, nsel), lambda i: (i, 0))],
        out_shape=[jax.ShapeDtypeStruct((n, nsel), I32),
                   jax.ShapeDtypeStruct((n, nsel), F32)],
        scratch_shapes=[pltpu.VMEM((2 * PEER_HEADS, PEER_NKEYS, tn), F32),
                        pltpu.VMEM((PEER_HEADS, PEER_TOPK, tn), F32),
                        pltpu.VMEM((PEER_HEADS, PEER_TOPK, tn), F32)],
        compiler_params=_cparams("arbitrary"),
        name="peer_route",
    )(x1, wq, keys)


def _pack_table(t):
    ne = t.shape[0]
    bits = lax.bitcast_convert_type(t.astype(BF16), jnp.uint16).astype(jnp.uint32)
    bits = bits.reshape(ne, PEER_WORDS, 2, LANES)
    words = (bits[:, :, 1, :] << 16) | bits[:, :, 0, :]
    return lax.bitcast_convert_type(words, I32).reshape(ne * PEER_WORDS, LANES)


def _gather_rows(idx_ref, tab_ref, stage, t):
    nsel = PEER_HEADS * PEER_TOPK
    for k in range(nsel):
        i = pl.multiple_of(idx_ref[0, 0, t * nsel + k], PEER_WORDS)
        stage[PEER_WORDS * k:PEER_WORDS * (k + 1), :] = tab_ref[pl.ds(i, PEER_WORDS), :]


def _unpack_rows(stage):
    w = stage[...]
    lo = lax.bitcast_convert_type(w << 16, F32).astype(BF16)
    hi = lax.bitcast_convert_type(w & jnp.int32(-65536), F32).astype(BF16)
    return lo, hi


def _token_pipeline(idx_ref, tab_ref, stages, tn, compute):
    _gather_rows(idx_ref, tab_ref, stages[0], 0)

    def group(i, carry):
        t0 = PEER_UNROLL * i
        for u in range(PEER_UNROLL):
            _gather_rows(idx_ref, tab_ref, stages[(u + 1) % 2], jnp.minimum(t0 + u + 1, tn - 1))
            compute(stages[u % 2], t0 + u)
        return carry

    lax.fori_loop(0, tn // PEER_UNROLL, group, 0)


def _diag_masks():
    r = lax.broadcasted_iota(I32, (SUBLANES, PEER_STAGE), 0)
    j = lax.broadcasted_iota(I32, (SUBLANES, PEER_STAGE), 1) & (PEER_WORDS - 1)
    return r == 2 * j, r == 2 * j + 1


def _peer_u_kernel(idx_ref, tab_ref, x_ref, gate_ref, w_ref, stage_a, stage_b, hbuf):
    tn = x_ref.shape[0]
    m_lo, m_hi = _diag_masks()

    def compute(stage, t):
        lo, hi = _unpack_rows(stage)
        x8 = x_ref[t].astype(BF16)
        part = jnp.where(m_lo, _nt(x8, lo), 0.0) + jnp.where(m_hi, _nt(x8, hi), 0.0)
        hbuf[pl.ds(t, 1), :] = jnp.sum(part, axis=0, keepdims=True)

    _token_pipeline(idx_ref, tab_ref, (stage_a, stage_b), tn, compute)

    h = hbuf[...]
    n = lax.broadcasted_iota(I32, h.shape, 1)
    h = h + jnp.where((n & 1) == 0, pltpu.roll(h, PEER_STAGE - 1, 1), pltpu.roll(h, 1, 1))
    h = h + jnp.where((n & 2) == 0, pltpu.roll(h, PEER_STAGE - 2, 1), pltpu.roll(h, 2, 1))
    e_r = lax.broadcasted_iota(I32, (PEER_HEADS * PEER_TOPK, PEER_STAGE), 0)
    e_c = _div_pow2(lax.broadcasted_iota(I32, (PEER_HEADS * PEER_TOPK, PEER_STAGE), 1), PEER_WORDS)
    g4 = _mm(gate_ref[...], (e_r == e_c).astype(F32), precision=HIGHEST)
    w_ref[...] = g4 * jax.nn.gelu(h)


def _peer_v_kernel(idx_ref, tab_ref, w_ref, o_ref, stage_a, stage_b):
    tn = w_ref.shape[0]
    m_lo, m_hi = _diag_masks()

    def compute(stage, t):
        lo, hi = _unpack_rows(stage)
        w_t = w_ref[pl.ds(t, 1), :]
        l_lo = jnp.where(m_lo, w_t, 0.0).astype(BF16)
        l_hi = jnp.where(m_hi, w_t, 0.0).astype(BF16)
        o_ref[t] = _mm(l_lo, lo) + _mm(l_hi, hi)

    _token_pipeline(idx_ref, tab_ref, (stage_a, stage_b), tn, compute)


def _peer_specs(n, tn):
    nsel = PEER_HEADS * PEER_TOPK
    idx_spec = pl.BlockSpec((1, 1, tn * nsel), lambda i: (i, 0, 0), memory_space=pltpu.SMEM)
    tab_spec = pl.BlockSpec((PEER_N * PEER_WORDS, LANES), lambda i: (0, 0), pipeline_mode=pl.Buffered(1))
    return idx_spec, tab_spec


def _peer_u(eidx3, utab, x3, gates):
    n = x3.shape[0]
    tn = min(TN_PEER, n)
    nsel = PEER_HEADS * PEER_TOPK
    idx_spec, tab_spec = _peer_specs(n, tn)
    return pl.pallas_call(
        _peer_u_kernel,
        grid=(n // tn,),
        in_specs=[idx_spec, tab_spec,
                  pl.BlockSpec((tn, SUBLANES, LANES), lambda i: (i, 0, 0)),
                  pl.BlockSpec((tn, nsel), lambda i: (i, 0))],
        out_specs=pl.BlockSpec((tn, PEER_STAGE), lambda i: (i, 0)),
        out_shape=jax.ShapeDtypeStruct((n, PEER_STAGE), F32),
        scratch_shapes=[pltpu.VMEM((PEER_STAGE, LANES), I32),
                        pltpu.VMEM((PEER_STAGE, LANES), I32),
                        pltpu.VMEM((tn, PEER_STAGE), F32)],
        compiler_params=_cparams("arbitrary"),
        name="peer_u",
    )(eidx3, utab, x3, gates)


def _peer_v(eidx3, vtab, w4):
    n = w4.shape[0]
    tn = min(TN_PEER, n)
    idx_spec, tab_spec = _peer_specs(n, tn)
    return pl.pallas_call(
        _peer_v_kernel,
        grid=(n // tn,),
        in_specs=[idx_spec, tab_spec,
                  pl.BlockSpec((tn, PEER_STAGE), lambda i: (i, 0))],
        out_specs=pl.BlockSpec((tn, SUBLANES, LANES), lambda i: (i, 0, 0)),
        out_shape=jax.ShapeDtypeStruct((n, SUBLANES, LANES), F32),
        scratch_shapes=[pltpu.VMEM((PEER_STAGE, LANES), I32),
                        pltpu.VMEM((PEER_STAGE, LANES), I32)],
        compiler_params=_cparams("arbitrary"),
        name="peer_v",
    )(eidx3, vtab, w4)


def _ple_kernel(x_ref, peer_ref, p_ref, gw_ref, gb_ref, pw_ref, g_ref, beta_ref, o_ref):
    r = DEEPNORM_ALPHA * x_ref[...] + peer_ref[...]
    gate = _sigmoid(_mm(r.astype(BF16), gw_ref[...]) + gb_ref[...])
    r = r + gate * _mm(p_ref[...].astype(BF16), pw_ref[...])
    o_ref[...] = _layer_norm(r, g_ref[...], beta_ref[...])


def _ple(x1, peer_out, p_all, layer, gw, gb, pw, g, beta):
    n = x1.shape[0]
    tn = min(TN_PROJ, n)
    nb = n // tn
    tok = lambda w: pl.BlockSpec((tn, w), lambda i: (i, 0))
    full = lambda r, w: pl.BlockSpec((r, w), lambda i: (0, 0))
    return pl.pallas_call(
        _ple_kernel,
        grid=(nb,),
        in_specs=[tok(D_MODEL), tok(D_MODEL),
                  pl.BlockSpec((tn, PLE_DIM), lambda i: (layer * nb + i, 0)),
                  full(D_MODEL, D_MODEL), full(1, D_MODEL), full(PLE_DIM, D_MODEL),
                  full(1, D_MODEL), full(1, D_MODEL)],
        out_specs=tok(D_MODEL),
        out_shape=jax.ShapeDtypeStruct((n, D_MODEL), F32),
        compiler_params=_cparams("arbitrary"),
        name="ple_ln",
    )(x1, peer_out, p_all, gw, gb, pw, g, beta)


def _pad_cols(w, width):
    return jnp.pad(w, ((0, 0), (0, width - w.shape[1])))


def _layout_w_in(w):
    qk = GLA_HEADS * GLA_DK
    o = 0
    a = w[:, o:o + A_W]; o += A_W
    bq = w[:, o:o + qk]; o += qk
    bk = w[:, o:o + qk]; o += qk
    bv = w[:, o:o + GLA_VW]; o += GLA_VW
    bg = w[:, o:o + GLA_VW]; o += GLA_VW
    blr = w[:, o:o + GLA_GATE_RANK]; o += GLA_GATE_RANK
    c = w[:, o:o + C_W]; o += C_W
    cf = w[:, o:o + FOX_HEADS]
    wp = jnp.concatenate([a, _pad_cols(bq, GLA_DKP), _pad_cols(bk, GLA_DKP), bv, bg,
                          _pad_cols(blr, LANES), c], axis=1).astype(BF16)
    wft = jnp.pad(cf.T, ((0, SUBLANES - FOX_HEADS), (0, 0)))
    return wp, wft


def _layer(x, p_all, layer, bsz, seq, prm):
    (w_in, conv_w, conv_b, conv_ln_g, conv_ln_b, gla_gate_w, gla_gate_b, gla_norm_g, fox_forget_b,
     w_out, ln1_g, ln1_b, peer_wq, peer_keys, peer_u, peer_v, ple_w, ple_gw, ple_gb, ln2_g, ln2_b) = prm
    n = bsz * seq
    row = lambda v: v.reshape(1, -1)

    wp, wft = _layout_w_in(w_in)
    ha, hb, hc, ft = _inproj(x, wp, wft)

    a_out = _conv_module(ha, conv_w, row(conv_b), row(conv_ln_g), row(conv_ln_b), bsz, seq)

    w2p = jnp.pad(gla_gate_w, ((0, LANES - GLA_GATE_RANK), (0, GLA_DKP - GLA_HEADS * GLA_DK))).astype(BF16)
    b2p = _pad_cols(row(gla_gate_b), GLA_DKP)
    b_out = _gla(hb, w2p, b2p, row(gla_norm_g), bsz, seq)

    bfp = jnp.pad(fox_forget_b, (0, SUBLANES - FOX_HEADS)).reshape(SUBLANES, 1)
    c = _fox_cumsum(ft, bfp, bsz, seq)
    tq = min(TQ_FOX, seq)
    c_out = _fox(hc, c.reshape(SUBLANES // 2, 2, n // tq, tq), bsz, seq)

    wo = w_out.astype(BF16)
    x1 = _outproj(a_out, b_out, c_out, x, wo[0:CONV_CH], wo[CONV_CH:CONV_CH + GLA_VW],
                  wo[CONV_CH + GLA_VW:], row(ln1_g), row(ln1_b))

    keys = peer_keys.reshape(2 * PEER_HEADS, PEER_NKEYS, PEER_HALF).astype(BF16)
    eidx, gates = _route(x1, peer_wq.astype(BF16), keys)
    tn = min(TN_PEER, n)
    eidx3 = eidx.reshape(n // tn, 1, tn * PEER_HEADS * PEER_TOPK)
    w4 = _peer_u(eidx3, _pack_table(peer_u), x1.reshape(n, SUBLANES, LANES), gates)
    peer_out = _peer_v(eidx3, _pack_table(peer_v), w4).reshape(n, D_MODEL)

    return _ple(x1, peer_out, p_all, layer, ple_gw.astype(BF16), row(ple_gb), ple_w.astype(BF16),
                row(ln2_g), row(ln2_b))


def kernel(x, p, w_in, conv_w, conv_b, conv_ln_g, conv_ln_b, gla_gate_w, gla_gate_b, gla_norm_g, fox_forget_b, w_out, ln1_g, ln1_b, peer_wq, peer_keys, peer_u, peer_v, ple_w, ple_gw, ple_gb, ln2_g, ln2_b):
    bsz, seq, d = x.shape
    depth = p.shape[0]
    xf = x.reshape(bsz * seq, d)
    p_all = p.reshape(depth * bsz * seq, PLE_DIM)
    stacked = (w_in, conv_w, conv_b, conv_ln_g, conv_ln_b, gla_gate_w, gla_gate_b, gla_norm_g, fox_forget_b,
               w_out, ln1_g, ln1_b, peer_wq, peer_keys, peer_u, peer_v, ple_w, ple_gw, ple_gb, ln2_g, ln2_b)
    for i in range(depth):
        xf = _layer(xf, p_all, i, bsz, seq, tuple(t[i] for t in stacked))
    return xf.reshape(bsz, seq, d)
```

```python
import functools

import jax
import jax.numpy as jnp
from jax import lax
from jax.experimental import pallas as pl
from jax.experimental.pallas import tpu as pltpu

F32, BF16, I32 = jnp.float32, jnp.bfloat16, jnp.int32
HIGHEST = lax.Precision.HIGHEST

D_MODEL = 1024
DEPTH = 4
CHUNK = 64
PLE_DIM = 256
CONV_CH = 256
CONV_WIDTH = 31
GLA_HEADS = 6
GLA_DK = 32
GLA_DV = 64
GLA_GATE_RANK = 16
GLA_GATE_TAU = 16.0
FOX_HEADS = 6
FOX_DH = 64
PEER_HEADS = 8
PEER_NKEYS = 128
PEER_N = PEER_NKEYS * PEER_NKEYS
PEER_HALF = 128
PEER_TOPK = 16
DEEPNORM_ALPHA = (2.0 * DEPTH) ** 0.25
LN_EPS = 1e-5

LANES = 128
SUBLANES = 8
VMEM_LIMIT = 48 * 1024 * 1024

GLA_DKP = 256
GLA_VW = GLA_HEADS * GLA_DV
FOX_W = FOX_HEADS * FOX_DH
A_W = 2 * CONV_CH
B_W = 2 * GLA_DKP + 2 * GLA_VW + LANES
C_W = 3 * FOX_W
W_IN_COLS = A_W + B_W + C_W

TN_PROJ = 512
TS_CONV = 512
TQ_FOX = 256
TN_ROUTE = 256
TN_PEER = 64
PEER_WORDS = 4
PEER_STAGE = PEER_HEADS * PEER_TOPK * PEER_WORDS
PEER_UNROLL = 16


def _cparams(*sem):
    return pltpu.CompilerParams(dimension_semantics=sem, vmem_limit_bytes=VMEM_LIMIT)


def _nt(a, b, precision=None):
    return lax.dot_general(a, b, (((1,), (1,)), ((), ())),
                           preferred_element_type=F32, precision=precision)


def _mm(a, b, precision=None):
    return jnp.dot(a, b, preferred_element_type=F32, precision=precision)


def _layer_norm(r, g, b):
    mu = jnp.mean(r, axis=-1, keepdims=True)
    d = r - mu
    var = jnp.mean(d * d, axis=-1, keepdims=True)
    return d * lax.rsqrt(var + LN_EPS) * g + b


def _log_sigmoid(x):
    return jnp.minimum(x, 0.0) - jnp.log(1.0 + jnp.exp(-jnp.abs(x)))


def _sigmoid(x):
    return 1.0 / (1.0 + jnp.exp(-x))


def _div_pow2(i, d):
    assert d & (d - 1) == 0
    return lax.shift_right_logical(i, jnp.int32(d.bit_length() - 1))


def _inproj_kernel(x_ref, w_ref, wf_ref, ha_ref, hb_ref, hc_ref, ft_ref):
    x = x_ref[...]
    xb = x.astype(BF16)
    ha_ref[...] = _mm(xb, w_ref[:, 0:A_W])
    hb_ref[...] = _mm(xb, w_ref[:, A_W:A_W + B_W])
    hc_ref[...] = _mm(xb, w_ref[:, A_W + B_W:W_IN_COLS]).astype(BF16)
    ft_ref[...] = _nt(wf_ref[...], x, precision=HIGHEST)


def _inproj(x, wp, wft):
    n = x.shape[0]
    tn = min(TN_PROJ, n)
    return pl.pallas_call(
        _inproj_kernel,
        grid=(n // tn,),
        in_specs=[pl.BlockSpec((tn, D_MODEL), lambda i: (i, 0)),
                  pl.BlockSpec((D_MODEL, W_IN_COLS), lambda i: (0, 0)),
                  pl.BlockSpec((SUBLANES, D_MODEL), lambda i: (0, 0))],
        out_specs=[pl.BlockSpec((tn, A_W), lambda i: (i, 0)),
                   pl.BlockSpec((tn, B_W), lambda i: (i, 0)),
                   pl.BlockSpec((tn, C_W), lambda i: (i, 0)),
                   pl.BlockSpec((SUBLANES, tn), lambda i: (0, i))],
        out_shape=[jax.ShapeDtypeStruct((n, A_W), F32),
                   jax.ShapeDtypeStruct((n, B_W), F32),
                   jax.ShapeDtypeStruct((n, C_W), BF16),
                   jax.ShapeDtypeStruct((SUBLANES, n), F32)],
        compiler_params=_cparams("arbitrary"),
        name="inproj",
    )(x, wp, wft)


CONV_HALO = 32
CONV_SUB = 64


def _conv_kernel(ha_ref, w_ref, b_ref, g_ref, beta_ref, o_ref, ubuf):
    j = pl.program_id(1)
    ts = ha_ref.shape[0]

    @pl.when(j == 0)
    def _():
        ubuf[0:CONV_HALO, :] = jnp.zeros((CONV_HALO, CONV_CH), F32)

    @pl.when(j > 0)
    def _():
        ubuf[0:CONV_HALO, :] = ubuf[ts:ts + CONV_HALO, :]

    ubuf[CONV_HALO:CONV_HALO + ts, :] = ha_ref[:, 0:CONV_CH] * _sigmoid(ha_ref[:, CONV_CH:2 * CONV_CH])

    off = CONV_HALO - (CONV_WIDTH - 1)

    for base in range(0, ts, CONV_SUB):
        acc = jnp.zeros((CONV_SUB, CONV_CH), F32) + b_ref[...]
        for t in range(CONV_WIDTH):
            acc = acc + w_ref[t:t + 1, :] * ubuf[base + off + t:base + off + t + CONV_SUB, :]
        y = _layer_norm(acc, g_ref[...], beta_ref[...])
        o_ref[base:base + CONV_SUB, :] = (y * _sigmoid(y)).astype(BF16)


def _conv_module(ha, w, b, g, beta, bsz, seq):
    ts = min(TS_CONV, seq)
    nb = seq // ts
    return pl.pallas_call(
        _conv_kernel,
        grid=(bsz, nb),
        in_specs=[pl.BlockSpec((ts, A_W), lambda bi, j: (bi * nb + j, 0)),
                  pl.BlockSpec((CONV_WIDTH, CONV_CH), lambda bi, j: (0, 0)),
                  pl.BlockSpec((1, CONV_CH), lambda bi, j: (0, 0)),
                  pl.BlockSpec((1, CONV_CH), lambda bi, j: (0, 0)),
                  pl.BlockSpec((1, CONV_CH), lambda bi, j: (0, 0))],
        out_specs=pl.BlockSpec((ts, CONV_CH), lambda bi, j: (bi * nb + j, 0)),
        out_shape=jax.ShapeDtypeStruct((bsz * seq, CONV_CH), BF16),
        scratch_shapes=[pltpu.VMEM((CONV_HALO + ts, CONV_CH), F32)],
        compiler_params=_cparams("arbitrary", "arbitrary"),
        name="conv_module",
    )(ha, w, b, g, beta)


GLA_ROW_BLOCK = 16


def _gla_kernel(hb_ref, w2_ref, b2_ref, ng_ref, o_ref, st_ref, abuf):
    L = CHUNK

    @pl.when(pl.program_id(1) == 0)
    def _():
        st_ref[...] = jnp.zeros(st_ref.shape, F32)

    q = hb_ref[:, 0:GLA_DKP] * (GLA_DK ** -0.5)
    k = hb_ref[:, GLA_DKP:2 * GLA_DKP]
    v = hb_ref[:, 2 * GLA_DKP:2 * GLA_DKP + GLA_VW]
    g_out = hb_ref[:, 2 * GLA_DKP + GLA_VW:2 * GLA_DKP + 2 * GLA_VW]
    lr = hb_ref[:, 2 * GLA_DKP + 2 * GLA_VW:B_W]

    z = _mm(lr.astype(BF16), w2_ref[...]) + b2_ref[...]
    gk = _log_sigmoid(z) * (1.0 / GLA_GATE_TAU)

    r_i = lax.broadcasted_iota(I32, (L, L), 0)
    c_i = lax.broadcasted_iota(I32, (L, L), 1)
    causal = r_i >= c_i
    cum = _mm(causal.astype(F32), gk, precision=HIGHEST)

    hm_r = lax.broadcasted_iota(I32, (SUBLANES, GLA_DKP), 0)
    hm_c = lax.broadcasted_iota(I32, (SUBLANES, GLA_DKP), 1)
    head_sel = _div_pow2(hm_c, GLA_DK) == hm_r

    vb = v.astype(BF16)

    abuf[...] = jnp.zeros(abuf.shape, F32)
    for i in range(L // GLA_ROW_BLOCK):
        nk = GLA_ROW_BLOCK * (i + 1)
        k_i, cum_i = k[0:nk, :], cum[0:nk, :]
        for t in range(GLA_ROW_BLOCK * i, nk):
            w = k_i * jnp.exp(jnp.minimum(cum[t:t + 1, :] - cum_i, 0.0))
            qm = jnp.where(head_sel, q[t:t + 1, :], 0.0)
            abuf[t * SUBLANES:(t + 1) * SUBLANES, 0:nk] = _nt(qm.astype(BF16), w.astype(BF16))

    st = st_ref[...]
    inter = _nt((q * jnp.exp(cum)).astype(BF16), st.astype(BF16))

    lane_head = _div_pow2(lax.broadcasted_iota(I32, (L, GLA_VW), 1), GLA_DV)
    o = inter
    for h in range(GLA_HEADS):
        a_h = abuf[pl.ds(h, L, stride=SUBLANES), :]
        a_h = jnp.where(causal, a_h, 0.0)
        p = _mm(a_h.astype(BF16), vb)
        o = o + jnp.where(lane_head == h, p, 0.0)

    last = cum[L - 1:L, :]
    kdec = k * jnp.exp(last - cum)
    upd = lax.dot_general(vb, kdec.astype(BF16), (((0,), (0,)), ((), ())), preferred_element_type=F32)
    s_r = _div_pow2(lax.broadcasted_iota(I32, st_ref.shape, 0), GLA_DV)
    s_c = _div_pow2(lax.broadcasted_iota(I32, st_ref.shape, 1), GLA_DK)
    st_ref[...] = st * jnp.exp(last) + jnp.where(s_r == s_c, upd, 0.0)

    o2 = o * o
    ms = jnp.zeros_like(o)
    for h in range(GLA_HEADS):
        sel = lane_head == h
        s = jnp.sum(jnp.where(sel, o2, 0.0), axis=-1, keepdims=True)
        ms = jnp.where(sel, s, ms)
    o = o * lax.rsqrt(ms * (1.0 / GLA_DV) + LN_EPS) * ng_ref[...]
    o_ref[...] = (o * (g_out * _sigmoid(g_out))).astype(BF16)


def _gla(hb, w2p, b2p, ng, bsz, seq):
    nc = seq // CHUNK
    return pl.pallas_call(
        _gla_kernel,
        grid=(bsz, nc),
        in_specs=[pl.BlockSpec((CHUNK, B_W), lambda bi, c: (bi * nc + c, 0)),
                  pl.BlockSpec((LANES, GLA_DKP), lambda bi, c: (0, 0)),
                  pl.BlockSpec((1, GLA_DKP), lambda bi, c: (0, 0)),
                  pl.BlockSpec((1, GLA_VW), lambda bi, c: (0, 0))],
        out_specs=pl.BlockSpec((CHUNK, GLA_VW), lambda bi, c: (bi * nc + c, 0)),
        out_shape=jax.ShapeDtypeStruct((bsz * seq, GLA_VW), BF16),
        scratch_shapes=[pltpu.VMEM((GLA_VW, GLA_DKP), F32),
                        pltpu.VMEM((CHUNK * SUBLANES, CHUNK), F32)],
        compiler_params=_cparams("arbitrary", "arbitrary"),
        name="gla",
    )(hb, w2p, b2p, ng)


def _fcum_kernel(ft_ref, bf_ref, c_ref):
    seq = ft_ref.shape[1]
    r_i = lax.broadcasted_iota(I32, (LANES, LANES), 0)
    c_i = lax.broadcasted_iota(I32, (LANES, LANES), 1)
    tri = (r_i <= c_i).astype(F32)
    carry = jnp.zeros((SUBLANES, 1), F32)
    for blk in range(seq // LANES):
        sl = slice(blk * LANES, (blk + 1) * LANES)
        ls = _log_sigmoid(ft_ref[:, sl] + bf_ref[...])
        cs = _mm(ls, tri, precision=HIGHEST) + carry
        c_ref[:, sl] = cs
        carry = cs[:, LANES - 1:LANES]


def _fox_cumsum(ft, bfp, bsz, seq):
    return pl.pallas_call(
        _fcum_kernel,
        grid=(bsz,),
        in_specs=[pl.BlockSpec((SUBLANES, seq), lambda bi: (0, bi)),
                  pl.BlockSpec((SUBLANES, 1), lambda bi: (0, 0))],
        out_specs=pl.BlockSpec((SUBLANES, seq), lambda bi: (0, bi)),
        out_shape=jax.ShapeDtypeStruct((SUBLANES, bsz * seq), F32),
        compiler_params=_cparams("arbitrary"),
        name="fox_cumsum",
    )(ft, bfp)


def _fox_kernel(q_ref, k_ref, v_ref, c_ref, o_ref, vt_sc, cb_sc):
    tq = q_ref.shape[0]
    nkb = v_ref.shape[0] // tq
    qi = pl.program_id(2)

    @pl.when(qi == 0)
    def _():
        def prep(kb, carry):
            k0 = pl.multiple_of(kb * tq, tq)
            vt_sc[kb] = jnp.transpose(v_ref[pl.ds(k0, tq), :].astype(F32)).astype(BF16)
            for j in range(2):
                c_row = c_ref[j, pl.ds(kb, 1), :]
                cb_sc[j * nkb + kb] = jnp.transpose(jnp.broadcast_to(c_row, (LANES, tq)))
            return carry

        lax.fori_loop(0, nkb, prep, 0)

    row = lax.broadcasted_iota(I32, (LANES, tq), 0)
    first = row < FOX_DH
    q_t = jnp.transpose(q_ref[...].astype(F32)) * (FOX_DH ** -0.5)
    q_heads = (jnp.where(first, q_t, 0.0).astype(BF16), jnp.where(first, 0.0, q_t).astype(BF16))
    c_q = tuple(c_ref[j, pl.ds(qi, 1), :] for j in range(2))
    s_i = lax.broadcasted_iota(I32, (tq, tq), 0)
    t_i = lax.broadcasted_iota(I32, (tq, tq), 1)
    causal = s_i <= t_i

    ng = tq // LANES
    chains = [(j, g) for j in range(2) for g in range(ng)]

    def scores(kb):
        k_blk = k_ref[pl.ds(pl.multiple_of(kb * tq, tq), tq), :]
        return tuple(_mm(k_blk, q_heads[j][:, g * LANES:(g + 1) * LANES]) for j, g in chains)

    def update(kb, qk, state, masked):
        v_t = vt_sc[kb]
        out = []
        for (j, g), s, (m, l, acc) in zip(chains, qk, state):
            sl = slice(g * LANES, (g + 1) * LANES)
            s = s + (c_q[j][:, sl] - cb_sc[j * nkb + kb])
            if masked:
                s = jnp.where(causal[:, sl], s, -jnp.inf)
            m_new = jnp.maximum(m, jnp.max(s, axis=0, keepdims=True))
            alpha = jnp.exp(m - m_new)
            p = jnp.exp(s - m_new)
            l = alpha * l + jnp.sum(p, axis=0, keepdims=True)
            acc = alpha * acc + _mm(v_t, p.astype(BF16))
            out.append((m_new, l, acc))
        return tuple(out)

    def step(kb, state):
        return update(kb, scores(kb), state, False)

    init = (jnp.full((1, LANES), -jnp.inf, F32), jnp.zeros((1, LANES), F32), jnp.zeros((LANES, LANES), F32))
    state = lax.fori_loop(0, qi, step, (init,) * len(chains))
    fin = update(qi, scores(qi), state, True)
    head = [jnp.concatenate([fin[j * ng + g][2] / fin[j * ng + g][1] for g in range(ng)], axis=1) for j in range(2)]
    o_ref[...] = jnp.transpose(jnp.where(first, head[0], head[1])).astype(BF16)


def _fox(hc, c4, bsz, seq):
    tq = min(TQ_FOX, seq)
    nq = seq // tq
    pairs = FOX_HEADS // 2
    return pl.pallas_call(
        _fox_kernel,
        grid=(bsz, pairs, nq),
        in_specs=[pl.BlockSpec((tq, LANES), lambda bi, p, i: (bi * nq + i, p)),
                  pl.BlockSpec((seq, LANES), lambda bi, p, i: (bi, pairs + p)),
                  pl.BlockSpec((seq, LANES), lambda bi, p, i: (bi, 2 * pairs + p)),
                  pl.BlockSpec((None, 2, nq, tq), lambda bi, p, i: (p, 0, bi, 0))],
        out_specs=pl.BlockSpec((tq, LANES), lambda bi, p, i: (bi * nq + i, p)),
        out_shape=jax.ShapeDtypeStruct((bsz * seq, FOX_W), BF16),
        scratch_shapes=[pltpu.VMEM((nq, LANES, tq), BF16),
                        pltpu.VMEM((2 * nq, tq, LANES), F32)],
        compiler_params=_cparams("arbitrary", "arbitrary", "arbitrary"),
        name="fox_attention",
    )(hc, hc, hc, c4)


def _outproj_kernel(a_ref, b_ref, c_ref, x_ref, wa_ref, wb_ref, wc_ref, g_ref, beta_ref, o_ref):
    mix = _mm(a_ref[...], wa_ref[...]) + _mm(b_ref[...], wb_ref[...]) + _mm(c_ref[...], wc_ref[...])
    o_ref[...] = _layer_norm(DEEPNORM_ALPHA * x_ref[...] + mix, g_ref[...], beta_ref[...])


def _outproj(a, b, c, x, wa, wb, wc, g, beta):
    n = x.shape[0]
    tn = min(TN_PROJ, n)
    tok = lambda w: pl.BlockSpec((tn, w), lambda i: (i, 0))
    full = lambda r, w: pl.BlockSpec((r, w), lambda i: (0, 0))
    return pl.pallas_call(
        _outproj_kernel,
        grid=(n // tn,),
        in_specs=[tok(CONV_CH), tok(GLA_VW), tok(FOX_W), tok(D_MODEL),
                  full(CONV_CH, D_MODEL), full(GLA_VW, D_MODEL), full(FOX_W, D_MODEL),
                  full(1, D_MODEL), full(1, D_MODEL)],
        out_specs=tok(D_MODEL),
        out_shape=jax.ShapeDtypeStruct((n, D_MODEL), F32),
        compiler_params=_cparams("arbitrary"),
        name="outproj_ln",
    )(a, b, c, x, wa, wb, wc, g, beta)


def _topk_rows(sc, kk):
    nrow = sc.shape[0]
    rows = lax.broadcasted_iota(I32, sc.shape, 0).astype(F32)
    vals, idxs = [], []
    for _ in range(kk):
        m = jnp.max(sc, axis=0, keepdims=True)
        idx = jnp.min(jnp.where(sc == m, rows, float(nrow)), axis=0, keepdims=True)
        vals.append(m)
        idxs.append(idx)
        sc = jnp.where(rows == idx, -jnp.inf, sc)
    return jnp.concatenate(vals, axis=0), jnp.concatenate(idxs, axis=0)


def _take_rows(table, sel, kk):
    out = jnp.zeros_like(sel)
    for a in range(kk):
        out = jnp.where(sel == float(a), table[a:a + 1, :], out)
    return out


def _candidate_sums(sv1, sv2):
    half = PEER_TOPK // 2
    blocks = [sv1[0:1, :] + sv2]
    blocks += [sv1[a:a + 1, :] + sv2[0:half, :] for a in range(1, half)]
    blocks += [sv1[half:, :] + sv2[0:1, :]]
    cand = jnp.concatenate(blocks, axis=0)
    n0, n1 = PEER_TOPK, PEER_TOPK + half * (half - 1)

    def a_of(r):
        q = jnp.floor((r - n0) * (1.0 / half))
        return jnp.where(r < n0, 0.0, jnp.where(r < n1, 1.0 + q, r - (n1 - half)))

    def b_of(r):
        q = jnp.floor((r - n0) * (1.0 / half))
        return jnp.where(r < n0, r, jnp.where(r < n1, r - n0 - half * q, 0.0))

    return cand, a_of, b_of


def _route_kernel(x_ref, wq_ref, keys_ref, eidx_ref, gate_ref, sc_ref, e_sc, g_sc):
    tn = x_ref.shape[0]
    xq = _mm(x_ref[...].astype(BF16), wq_ref[...])
    for hc in range(2 * PEER_HEADS):
        xq_hc = xq[:, hc * PEER_HALF:(hc + 1) * PEER_HALF].astype(BF16)
        sc_ref[hc] = _nt(keys_ref[hc], xq_hc)

    def head(h, carry):
        for half in range(tn // LANES):
            sl = slice(half * LANES, (half + 1) * LANES)
            sv1, si1 = _topk_rows(sc_ref[2 * h, :, sl], PEER_TOPK)
            sv2, si2 = _topk_rows(sc_ref[2 * h + 1, :, sl], PEER_TOPK)
            cand, a_sel, b_sel = _candidate_sums(sv1, sv2)
            cv, ci = _topk_rows(cand, PEER_TOPK)
            a_sel, b_sel = a_sel(ci), b_sel(ci)
            i1 = _take_rows(si1, a_sel, PEER_TOPK)
            i2 = _take_rows(si2, b_sel, PEER_TOPK)
            e = jnp.exp(cv - cv[0:1, :])
            e_sc[h, :, sl] = i1 * float(PEER_NKEYS) + i2
            g_sc[h, :, sl] = e / jnp.sum(e, axis=0, keepdims=True)
        return carry

    lax.fori_loop(0, PEER_HEADS, head, 0)
    nsel = PEER_HEADS * PEER_TOPK
    e_all = e_sc[...].reshape(nsel, tn)
    g_all = g_sc[...].reshape(nsel, tn)
    eidx_ref[...] = jnp.transpose(e_all * float(PEER_WORDS)).astype(I32)
    gate_ref[...] = jnp.transpose(g_all)


def _route(x1, wq, keys):
    n = x1.shape[0]
    tn = min(TN_ROUTE, n)
    nsel = PEER_HEADS * PEER_TOPK
    qw = PEER_HEADS * 2 * PEER_HALF
    return pl.pallas_call(
        _route_kernel,
        grid=(n // tn,),
        in_specs=[pl.BlockSpec((tn, D_MODEL), lambda i: (i, 0)),
                  pl.BlockSpec((D_MODEL, qw), lambda i: (0, 0)),
                  pl.BlockSpec((2 * PEER_HEADS, PEER_NKEYS, PEER_HALF), lambda i: (0, 0, 0))],
        out_specs=[pl.BlockSpec((tn, nsel), lambda i: (i, 0)),
                   pl.BlockSpec((tn, nsel), lambda i: (i, 0))],
        out_shape=[jax.ShapeDtypeStruct((n, nsel), I32),
                   jax.ShapeDtypeStruct((n, nsel), F32)],
        scratch_shapes=[pltpu.VMEM((2 * PEER_HEADS, PEER_NKEYS, tn), F32),
                        pltpu.VMEM((PEER_HEADS, PEER_TOPK, tn), F32),
                        pltpu.VMEM((PEER_HEADS, PEER_TOPK, tn), F32)],
        compiler_params=_cparams("arbitrary"),
        name="peer_route",
    )(x1, wq, keys)


PACK_ROWS = 512


def _pack_kernel(t_ref, o_ref):
    for j in range(PEER_WORDS):
        lo = t_ref[:, (2 * j) * LANES:(2 * j + 1) * LANES].astype(BF16).astype(F32)
        hi = t_ref[:, (2 * j + 1) * LANES:(2 * j + 2) * LANES].astype(BF16).astype(F32)
        lo_bits = lax.shift_right_logical(lax.bitcast_convert_type(lo, I32), jnp.int32(16))
        hi_bits = lax.bitcast_convert_type(hi, I32) & jnp.int32(-65536)
        o_ref[pl.ds(j, t_ref.shape[0], stride=PEER_WORDS), :] = hi_bits | lo_bits


def _pack_table(t):
    ne = t.shape[0]
    rows = min(PACK_ROWS, ne)
    return pl.pallas_call(
        _pack_kernel,
        grid=(ne // rows,),
        in_specs=[pl.BlockSpec((rows, D_MODEL), lambda i: (i, 0))],
        out_specs=pl.BlockSpec((rows * PEER_WORDS, LANES), lambda i: (i, 0)),
        out_shape=jax.ShapeDtypeStruct((ne * PEER_WORDS, LANES), I32),
        compiler_params=_cparams("arbitrary"),
        name="peer_pack",
    )(t)


def _gather_rows(idx_ref, tab_ref, stage, t):
    nsel = PEER_HEADS * PEER_TOPK
    for k in range(nsel):
        i = pl.multiple_of(idx_ref[0, 0, t * nsel + k], PEER_WORDS)
        stage[PEER_WORDS * k:PEER_WORDS * (k + 1), :] = tab_ref[pl.ds(i, PEER_WORDS), :]


def _unpack_rows(stage):
    w = stage[...]
    lo = lax.bitcast_convert_type(w << 16, F32).astype(BF16)
    hi = lax.bitcast_convert_type(w & jnp.int32(-65536), F32).astype(BF16)
    return lo, hi


def _token_pipeline(idx_ref, tab_ref, stages, tn, compute):
    _gather_rows(idx_ref, tab_ref, stages[0], 0)

    def group(i, carry):
        t0 = PEER_UNROLL * i
        for u in range(PEER_UNROLL):
            _gather_rows(idx_ref, tab_ref, stages[(u + 1) % 2], jnp.minimum(t0 + u + 1, tn - 1))
            compute(stages[u % 2], t0 + u)
        return carry

    lax.fori_loop(0, tn // PEER_UNROLL, group, 0)


def _diag_masks():
    r = lax.broadcasted_iota(I32, (SUBLANES, PEER_STAGE), 0)
    j = lax.broadcasted_iota(I32, (SUBLANES, PEER_STAGE), 1) & (PEER_WORDS - 1)
    return r == 2 * j, r == 2 * j + 1


def _peer_u_kernel(idx_ref, tab_ref, x_ref, gate_ref, w_ref, stage_a, stage_b, hbuf):
    tn = x_ref.shape[0]
    m_lo, m_hi = _diag_masks()

    def compute(stage, t):
        lo, hi = _unpack_rows(stage)
        x8 = x_ref[t].astype(BF16)
        part = jnp.where(m_lo, _nt(x8, lo), 0.0) + jnp.where(m_hi, _nt(x8, hi), 0.0)
        hbuf[pl.ds(t, 1), :] = jnp.sum(part, axis=0, keepdims=True)

    _token_pipeline(idx_ref, tab_ref, (stage_a, stage_b), tn, compute)

    h = hbuf[...]
    n = lax.broadcasted_iota(I32, h.shape, 1)
    h = h + jnp.where((n & 1) == 0, pltpu.roll(h, PEER_STAGE - 1, 1), pltpu.roll(h, 1, 1))
    h = h + jnp.where((n & 2) == 0, pltpu.roll(h, PEER_STAGE - 2, 1), pltpu.roll(h, 2, 1))
    e_r = lax.broadcasted_iota(I32, (PEER_HEADS * PEER_TOPK, PEER_STAGE), 0)
    e_c = _div_pow2(lax.broadcasted_iota(I32, (PEER_HEADS * PEER_TOPK, PEER_STAGE), 1), PEER_WORDS)
    g4 = _mm(gate_ref[...], (e_r == e_c).astype(F32), precision=HIGHEST)
    w_ref[...] = g4 * jax.nn.gelu(h)


def _peer_v_kernel(idx_ref, tab_ref, w_ref, o_ref, stage_a, stage_b):
    tn = w_ref.shape[0]
    m_lo, m_hi = _diag_masks()

    def compute(stage, t):
        lo, hi = _unpack_rows(stage)
        w_t = w_ref[pl.ds(t, 1), :]
        l_lo = jnp.where(m_lo, w_t, 0.0).astype(BF16)
        l_hi = jnp.where(m_hi, w_t, 0.0).astype(BF16)
        o_ref[t] = _mm(l_lo, lo) + _mm(l_hi, hi)

    _token_pipeline(idx_ref, tab_ref, (stage_a, stage_b), tn, compute)


def _peer_specs(n, tn):
    nsel = PEER_HEADS * PEER_TOPK
    idx_spec = pl.BlockSpec((1, 1, tn * nsel), lambda i: (i, 0, 0), memory_space=pltpu.SMEM)
    tab_spec = pl.BlockSpec((PEER_N * PEER_WORDS, LANES), lambda i: (0, 0), pipeline_mode=pl.Buffered(1))
    return idx_spec, tab_spec


def _peer_u(eidx3, utab, x3, gates):
    n = x3.shape[0]
    tn = min(TN_PEER, n)
    nsel = PEER_HEADS * PEER_TOPK
    idx_spec, tab_spec = _peer_specs(n, tn)
    return pl.pallas_call(
        _peer_u_kernel,
        grid=(n // tn,),
        in_specs=[idx_spec, tab_spec,
                  pl.BlockSpec((tn, SUBLANES, LANES), lambda i: (i, 0, 0)),
                  pl.BlockSpec((tn, nsel), lambda i: (i, 0))],
        out_specs=pl.BlockSpec((tn, PEER_STAGE), lambda i: (i, 0)),
        out_shape=jax.ShapeDtypeStruct((n, PEER_STAGE), F32),
        scratch_shapes=[pltpu.VMEM((PEER_STAGE, LANES), I32),
                        pltpu.VMEM((PEER_STAGE, LANES), I32),
                        pltpu.VMEM((tn, PEER_STAGE), F32)],
        compiler_params=_cparams("arbitrary"),
        name="peer_u",
    )(eidx3, utab, x3, gates)


def _peer_v(eidx3, vtab, w4):
    n = w4.shape[0]
    tn = min(TN_PEER, n)
    idx_spec, tab_spec = _peer_specs(n, tn)
    return pl.pallas_call(
        _peer_v_kernel,
        grid=(n // tn,),
        in_specs=[idx_spec, tab_spec,
                  pl.BlockSpec((tn, PEER_STAGE), lambda i: (i, 0))],
        out_specs=pl.BlockSpec((tn, SUBLANES, LANES), lambda i: (i, 0, 0)),
        out_shape=jax.ShapeDtypeStruct((n, SUBLANES, LANES), F32),
        scratch_shapes=[pltpu.VMEM((PEER_STAGE, LANES), I32),
                        pltpu.VMEM((PEER_STAGE, LANES), I32)],
        compiler_params=_cparams("arbitrary"),
        name="peer_v",
    )(eidx3, vtab, w4)


def _ple_kernel(x_ref, peer_ref, p_ref, gw_ref, gb_ref, pw_ref, g_ref, beta_ref, o_ref):
    r = DEEPNORM_ALPHA * x_ref[...] + peer_ref[...]
    gate = _sigmoid(_mm(r.astype(BF16), gw_ref[...]) + gb_ref[...])
    r = r + gate * _mm(p_ref[...].astype(BF16), pw_ref[...])
    o_ref[...] = _layer_norm(r, g_ref[...], beta_ref[...])


def _ple(x1, peer_out, p_all, layer, gw, gb, pw, g, beta):
    n = x1.shape[0]
    tn = min(TN_PROJ, n)
    nb = n // tn
    tok = lambda w: pl.BlockSpec((tn, w), lambda i: (i, 0))
    full = lambda r, w: pl.BlockSpec((r, w), lambda i: (0, 0))
    return pl.pallas_call(
        _ple_kernel,
        grid=(nb,),
        in_specs=[tok(D_MODEL), tok(D_MODEL),
                  pl.BlockSpec((tn, PLE_DIM), lambda i: (layer * nb + i, 0)),
                  full(D_MODEL, D_MODEL), full(1, D_MODEL), full(PLE_DIM, D_MODEL),
                  full(1, D_MODEL), full(1, D_MODEL)],
        out_specs=tok(D_MODEL),
        out_shape=jax.ShapeDtypeStruct((n, D_MODEL), F32),
        compiler_params=_cparams("arbitrary"),
        name="ple_ln",
    )(x1, peer_out, p_all, gw, gb, pw, g, beta)


def _pad_cols(w, width):
    return jnp.pad(w, ((0, 0), (0, width - w.shape[1])))


def _layout_w_in(w):
    qk = GLA_HEADS * GLA_DK
    o = 0
    a = w[:, o:o + A_W]; o += A_W
    bq = w[:, o:o + qk]; o += qk
    bk = w[:, o:o + qk]; o += qk
    bv = w[:, o:o + GLA_VW]; o += GLA_VW
    bg = w[:, o:o + GLA_VW]; o += GLA_VW
    blr = w[:, o:o + GLA_GATE_RANK]; o += GLA_GATE_RANK
    c = w[:, o:o + C_W]; o += C_W
    cf = w[:, o:o + FOX_HEADS]
    wp = jnp.concatenate([a, _pad_cols(bq, GLA_DKP), _pad_cols(bk, GLA_DKP), bv, bg,
                          _pad_cols(blr, LANES), c], axis=1).astype(BF16)
    wft = jnp.pad(cf.T, ((0, SUBLANES - FOX_HEADS), (0, 0)))
    return wp, wft


def _layer(x, p_all, layer, bsz, seq, prm):
    (w_in, conv_w, conv_b, conv_ln_g, conv_ln_b, gla_gate_w, gla_gate_b, gla_norm_g, fox_forget_b,
     w_out, ln1_g, ln1_b, peer_wq, peer_keys, peer_u, peer_v, ple_w, ple_gw, ple_gb, ln2_g, ln2_b) = prm
    n = bsz * seq
    row = lambda v: v.reshape(1, -1)

    wp, wft = _layout_w_in(w_in)
    ha, hb, hc, ft = _inproj(x, wp, wft)

    a_out = _conv_module(ha, conv_w, row(conv_b), row(conv_ln_g), row(conv_ln_b), bsz, seq)

    w2p = jnp.pad(gla_gate_w, ((0, LANES - GLA_GATE_RANK), (0, GLA_DKP - GLA_HEADS * GLA_DK))).astype(BF16)
    b2p = _pad_cols(row(gla_gate_b), GLA_DKP)
    b_out = _gla(hb, w2p, b2p, row(gla_norm_g), bsz, seq)

    bfp = jnp.pad(fox_forget_b, (0, SUBLANES - FOX_HEADS)).reshape(SUBLANES, 1)
    c = _fox_cumsum(ft, bfp, bsz, seq)
    tq = min(TQ_FOX, seq)
    c_out = _fox(hc, c.reshape(SUBLANES // 2, 2, n // tq, tq), bsz, seq)

    wo = w_out.astype(BF16)
    x1 = _outproj(a_out, b_out, c_out, x, wo[0:CONV_CH], wo[CONV_CH:CONV_CH + GLA_VW],
                  wo[CONV_CH + GLA_VW:], row(ln1_g), row(ln1_b))

    keys = peer_keys.reshape(2 * PEER_HEADS, PEER_NKEYS, PEER_HALF).astype(BF16)
    eidx, gates = _route(x1, peer_wq.astype(BF16), keys)
    tn = min(TN_PEER, n)
    eidx3 = eidx.reshape(n // tn, 1, tn * PEER_HEADS * PEER_TOPK)
    w4 = _peer_u(eidx3, _pack_table(peer_u), x1.reshape(n, SUBLANES, LANES), gates)
    peer_out = _peer_v(eidx3, _pack_table(peer_v), w4).reshape(n, D_MODEL)

    return _ple(x1, peer_out, p_all, layer, ple_gw.astype(BF16), row(ple_gb), ple_w.astype(BF16),
                row(ln2_g), row(ln2_b))


def kernel(x, p, w_in, conv_w, conv_b, conv_ln_g, conv_ln_b, gla_gate_w, gla_gate_b, gla_norm_g, fox_forget_b, w_out, ln1_g, ln1_b, peer_wq, peer_keys, peer_u, peer_v, ple_w, ple_gw, ple_gb, ln2_g, ln2_b):
    bsz, seq, d = x.shape
    depth = p.shape[0]
    xf = x.reshape(bsz * seq, d)
    p_all = p.reshape(depth * bsz * seq, PLE_DIM)
    stacked = (w_in, conv_w, conv_b, conv_ln_g, conv_ln_b, gla_gate_w, gla_gate_b, gla_norm_g, fox_forget_b,
               w_out, ln1_g, ln1_b, peer_wq, peer_keys, peer_u, peer_v, ple_w, ple_gw, ple_gb, ln2_g, ln2_b)
    for i in range(depth):
        xf = _layer(xf, p_all, i, bsz, seq, tuple(t[i] for t in stacked))
    return xf.reshape(bsz, seq, d)
```

```python
import functools

import jax
import jax.numpy as jnp
from jax import lax
from jax.experimental import pallas as pl
from jax.experimental.pallas import tpu as pltpu

F32, BF16, I32 = jnp.float32, jnp.bfloat16, jnp.int32
HIGHEST = lax.Precision.HIGHEST

D_MODEL = 1024
DEPTH = 4
CHUNK = 64
PLE_DIM = 256
CONV_CH = 256
CONV_WIDTH = 31
GLA_HEADS = 6
GLA_DK = 32
GLA_DV = 64
GLA_GATE_RANK = 16
GLA_GATE_TAU = 16.0
FOX_HEADS = 6
FOX_DH = 64
PEER_HEADS = 8
PEER_NKEYS = 128
PEER_N = PEER_NKEYS * PEER_NKEYS
PEER_HALF = 128
PEER_TOPK = 16
DEEPNORM_ALPHA = (2.0 * DEPTH) ** 0.25
LN_EPS = 1e-5

LANES = 128
SUBLANES = 8
VMEM_LIMIT = 48 * 1024 * 1024

GLA_DKP = 256
GLA_VW = GLA_HEADS * GLA_DV
FOX_W = FOX_HEADS * FOX_DH
A_W = 2 * CONV_CH
B_W = 2 * GLA_DKP + 2 * GLA_VW + LANES
C_W = 3 * FOX_W
W_IN_COLS = A_W + B_W + C_W

TN_PROJ = 512
TS_CONV = 512
TQ_FOX = 256
TN_FUSED = 128
TN_PEER = 64
PEER_WORDS = 4
PEER_STAGE = PEER_HEADS * PEER_TOPK * PEER_WORDS
PEER_UNROLL = 16


def _cparams(*sem):
    return pltpu.CompilerParams(dimension_semantics=sem, vmem_limit_bytes=VMEM_LIMIT)


def _nt(a, b, precision=None):
    return lax.dot_general(a, b, (((1,), (1,)), ((), ())),
                           preferred_element_type=F32, precision=precision)


def _mm(a, b, precision=None):
    return jnp.dot(a, b, preferred_element_type=F32, precision=precision)


def _layer_norm(r, g, b):
    mu = jnp.mean(r, axis=-1, keepdims=True)
    d = r - mu
    var = jnp.mean(d * d, axis=-1, keepdims=True)
    return d * lax.rsqrt(var + LN_EPS) * g + b


def _log_sigmoid(x):
    return jnp.minimum(x, 0.0) - jnp.log(1.0 + jnp.exp(-jnp.abs(x)))


def _sigmoid(x):
    return 1.0 / (1.0 + jnp.exp(-x))


def _div_pow2(i, d):
    assert d & (d - 1) == 0
    return lax.shift_right_logical(i, jnp.int32(d.bit_length() - 1))


def _inproj_kernel(x_ref, w_ref, wf_ref, ha_ref, hb_ref, hc_ref, ft_ref):
    x = x_ref[...]
    xb = x.astype(BF16)
    ha_ref[...] = _mm(xb, w_ref[:, 0:A_W])
    hb_ref[...] = _mm(xb, w_ref[:, A_W:A_W + B_W])
    hc_ref[...] = _mm(xb, w_ref[:, A_W + B_W:W_IN_COLS]).astype(BF16)
    ft_ref[...] = _nt(wf_ref[...], x, precision=HIGHEST)


def _inproj(x, wp, wft):
    n = x.shape[0]
    tn = min(TN_PROJ, n)
    return pl.pallas_call(
        _inproj_kernel,
        grid=(n // tn,),
        in_specs=[pl.BlockSpec((tn, D_MODEL), lambda i: (i, 0)),
                  pl.BlockSpec((D_MODEL, W_IN_COLS), lambda i: (0, 0)),
                  pl.BlockSpec((SUBLANES, D_MODEL), lambda i: (0, 0))],
        out_specs=[pl.BlockSpec((tn, A_W), lambda i: (i, 0)),
                   pl.BlockSpec((tn, B_W), lambda i: (i, 0)),
                   pl.BlockSpec((tn, C_W), lambda i: (i, 0)),
                   pl.BlockSpec((SUBLANES, tn), lambda i: (0, i))],
        out_shape=[jax.ShapeDtypeStruct((n, A_W), F32),
                   jax.ShapeDtypeStruct((n, B_W), F32),
                   jax.ShapeDtypeStruct((n, C_W), BF16),
                   jax.ShapeDtypeStruct((SUBLANES, n), F32)],
        compiler_params=_cparams("arbitrary"),
        name="inproj",
    )(x, wp, wft)


CONV_HALO = 32
CONV_SUB = 64


def _conv_kernel(ha_ref, w_ref, b_ref, g_ref, beta_ref, o_ref, ubuf):
    j = pl.program_id(1)
    ts = ha_ref.shape[0]

    @pl.when(j == 0)
    def _():
        ubuf[0:CONV_HALO, :] = jnp.zeros((CONV_HALO, CONV_CH), F32)

    @pl.when(j > 0)
    def _():
        ubuf[0:CONV_HALO, :] = ubuf[ts:ts + CONV_HALO, :]

    ubuf[CONV_HALO:CONV_HALO + ts, :] = ha_ref[:, 0:CONV_CH] * _sigmoid(ha_ref[:, CONV_CH:2 * CONV_CH])

    off = CONV_HALO - (CONV_WIDTH - 1)

    for base in range(0, ts, CONV_SUB):
        acc = jnp.zeros((CONV_SUB, CONV_CH), F32) + b_ref[...]
        for t in range(CONV_WIDTH):
            acc = acc + w_ref[t:t + 1, :] * ubuf[base + off + t:base + off + t + CONV_SUB, :]
        y = _layer_norm(acc, g_ref[...], beta_ref[...])
        o_ref[base:base + CONV_SUB, :] = (y * _sigmoid(y)).astype(BF16)


def _conv_module(ha, w, b, g, beta, bsz, seq):
    ts = min(TS_CONV, seq)
    nb = seq // ts
    return pl.pallas_call(
        _conv_kernel,
        grid=(bsz, nb),
        in_specs=[pl.BlockSpec((ts, A_W), lambda bi, j: (bi * nb + j, 0)),
                  pl.BlockSpec((CONV_WIDTH, CONV_CH), lambda bi, j: (0, 0)),
                  pl.BlockSpec((1, CONV_CH), lambda bi, j: (0, 0)),
                  pl.BlockSpec((1, CONV_CH), lambda bi, j: (0, 0)),
                  pl.BlockSpec((1, CONV_CH), lambda bi, j: (0, 0))],
        out_specs=pl.BlockSpec((ts, CONV_CH), lambda bi, j: (bi * nb + j, 0)),
        out_shape=jax.ShapeDtypeStruct((bsz * seq, CONV_CH), BF16),
        scratch_shapes=[pltpu.VMEM((CONV_HALO + ts, CONV_CH), F32)],
        compiler_params=_cparams("arbitrary", "arbitrary"),
        name="conv_module",
    )(ha, w, b, g, beta)


GLA_ROW_BLOCK = 16


def _gla_kernel(hb_ref, w2_ref, b2_ref, ng_ref, o_ref, st_ref, abuf):
    L = CHUNK

    @pl.when(pl.program_id(1) == 0)
    def _():
        st_ref[...] = jnp.zeros(st_ref.shape, F32)

    q = hb_ref[:, 0:GLA_DKP] * (GLA_DK ** -0.5)
    k = hb_ref[:, GLA_DKP:2 * GLA_DKP]
    v = hb_ref[:, 2 * GLA_DKP:2 * GLA_DKP + GLA_VW]
    g_out = hb_ref[:, 2 * GLA_DKP + GLA_VW:2 * GLA_DKP + 2 * GLA_VW]
    lr = hb_ref[:, 2 * GLA_DKP + 2 * GLA_VW:B_W]

    z = _mm(lr.astype(BF16), w2_ref[...]) + b2_ref[...]
    gk = _log_sigmoid(z) * (1.0 / GLA_GATE_TAU)

    r_i = lax.broadcasted_iota(I32, (L, L), 0)
    c_i = lax.broadcasted_iota(I32, (L, L), 1)
    causal = r_i >= c_i
    cum = _mm(causal.astype(F32), gk, precision=HIGHEST)

    hm_r = lax.broadcasted_iota(I32, (SUBLANES, GLA_DKP), 0)
    hm_c = lax.broadcasted_iota(I32, (SUBLANES, GLA_DKP), 1)
    head_sel = _div_pow2(hm_c, GLA_DK) == hm_r

    vb = v.astype(BF16)

    abuf[...] = jnp.zeros(abuf.shape, F32)
    for i in range(L // GLA_ROW_BLOCK):
        nk = GLA_ROW_BLOCK * (i + 1)
        k_i, cum_i = k[0:nk, :], cum[0:nk, :]
        for t in range(GLA_ROW_BLOCK * i, nk):
            w = k_i * jnp.exp(jnp.minimum(cum[t:t + 1, :] - cum_i, 0.0))
            qm = jnp.where(head_sel, q[t:t + 1, :], 0.0)
            abuf[t * SUBLANES:(t + 1) * SUBLANES, 0:nk] = _nt(qm.astype(BF16), w.astype(BF16))

    st = st_ref[...]
    inter = _nt((q * jnp.exp(cum)).astype(BF16), st.astype(BF16))

    lane_head = _div_pow2(lax.broadcasted_iota(I32, (L, GLA_VW), 1), GLA_DV)
    o = inter
    for h in range(GLA_HEADS):
        a_h = abuf[pl.ds(h, L, stride=SUBLANES), :]
        a_h = jnp.where(causal, a_h, 0.0)
        p = _mm(a_h.astype(BF16), vb)
        o = o + jnp.where(lane_head == h, p, 0.0)

    last = cum[L - 1:L, :]
    kdec = k * jnp.exp(last - cum)
    upd = lax.dot_general(vb, kdec.astype(BF16), (((0,), (0,)), ((), ())), preferred_element_type=F32)
    s_r = _div_pow2(lax.broadcasted_iota(I32, st_ref.shape, 0), GLA_DV)
    s_c = _div_pow2(lax.broadcasted_iota(I32, st_ref.shape, 1), GLA_DK)
    st_ref[...] = st * jnp.exp(last) + jnp.where(s_r == s_c, upd, 0.0)

    o2 = o * o
    ms = jnp.zeros_like(o)
    for h in range(GLA_HEADS):
        sel = lane_head == h
        s = jnp.sum(jnp.where(sel, o2, 0.0), axis=-1, keepdims=True)
        ms = jnp.where(sel, s, ms)
    o = o * lax.rsqrt(ms * (1.0 / GLA_DV) + LN_EPS) * ng_ref[...]
    o_ref[...] = (o * (g_out * _sigmoid(g_out))).astype(BF16)


def _gla(hb, w2p, b2p, ng, bsz, seq):
    nc = seq // CHUNK
    return pl.pallas_call(
        _gla_kernel,
        grid=(bsz, nc),
        in_specs=[pl.BlockSpec((CHUNK, B_W), lambda bi, c: (bi * nc + c, 0)),
                  pl.BlockSpec((LANES, GLA_DKP), lambda bi, c: (0, 0)),
                  pl.BlockSpec((1, GLA_DKP), lambda bi, c: (0, 0)),
                  pl.BlockSpec((1, GLA_VW), lambda bi, c: (0, 0))],
        out_specs=pl.BlockSpec((CHUNK, GLA_VW), lambda bi, c: (bi * nc + c, 0)),
        out_shape=jax.ShapeDtypeStruct((bsz * seq, GLA_VW), BF16),
        scratch_shapes=[pltpu.VMEM((GLA_VW, GLA_DKP), F32),
                        pltpu.VMEM((CHUNK * SUBLANES, CHUNK), F32)],
        compiler_params=_cparams("arbitrary", "arbitrary"),
        name="gla",
    )(hb, w2p, b2p, ng)


def _fcum_kernel(ft_ref, bf_ref, c_ref):
    seq = ft_ref.shape[1]
    r_i = lax.broadcasted_iota(I32, (LANES, LANES), 0)
    c_i = lax.broadcasted_iota(I32, (LANES, LANES), 1)
    tri = (r_i <= c_i).astype(F32)
    carry = jnp.zeros((SUBLANES, 1), F32)
    for blk in range(seq // LANES):
        sl = slice(blk * LANES, (blk + 1) * LANES)
        ls = _log_sigmoid(ft_ref[:, sl] + bf_ref[...])
        cs = _mm(ls, tri, precision=HIGHEST) + carry
        c_ref[:, sl] = cs
        carry = cs[:, LANES - 1:LANES]


def _fox_cumsum(ft, bfp, bsz, seq):
    return pl.pallas_call(
        _fcum_kernel,
        grid=(bsz,),
        in_specs=[pl.BlockSpec((SUBLANES, seq), lambda bi: (0, bi)),
                  pl.BlockSpec((SUBLANES, 1), lambda bi: (0, 0))],
        out_specs=pl.BlockSpec((SUBLANES, seq), lambda bi: (0, bi)),
        out_shape=jax.ShapeDtypeStruct((SUBLANES, bsz * seq), F32),
        compiler_params=_cparams("arbitrary"),
        name="fox_cumsum",
    )(ft, bfp)


def _fox_kernel(q_ref, k_ref, v_ref, c_ref, o_ref, vt_sc, cb_sc):
    tq = q_ref.shape[0]
    nkb = v_ref.shape[0] // tq
    qi = pl.program_id(2)

    @pl.when(qi == 0)
    def _():
        def prep(kb, carry):
            k0 = pl.multiple_of(kb * tq, tq)
            vt_sc[kb] = jnp.transpose(v_ref[pl.ds(k0, tq), :].astype(F32)).astype(BF16)
            for j in range(2):
                c_row = c_ref[j, pl.ds(kb, 1), :]
                cb_sc[j * nkb + kb] = jnp.transpose(jnp.broadcast_to(c_row, (LANES, tq)))
            return carry

        lax.fori_loop(0, nkb, prep, 0)

    row = lax.broadcasted_iota(I32, (LANES, tq), 0)
    first = row < FOX_DH
    q_t = jnp.transpose(q_ref[...].astype(F32)) * (FOX_DH ** -0.5)
    q_heads = (jnp.where(first, q_t, 0.0).astype(BF16), jnp.where(first, 0.0, q_t).astype(BF16))
    c_q = tuple(c_ref[j, pl.ds(qi, 1), :] for j in range(2))
    s_i = lax.broadcasted_iota(I32, (tq, tq), 0)
    t_i = lax.broadcasted_iota(I32, (tq, tq), 1)
    causal = s_i <= t_i

    ng = tq // LANES
    chains = [(j, g) for j in range(2) for g in range(ng)]

    def scores(kb):
        k_blk = k_ref[pl.ds(pl.multiple_of(kb * tq, tq), tq), :]
        return tuple(_mm(k_blk, q_heads[j][:, g * LANES:(g + 1) * LANES]) for j, g in chains)

    def update(kb, qk, state, masked):
        v_t = vt_sc[kb]
        out = []
        for (j, g), s, (m, l, acc) in zip(chains, qk, state):
            sl = slice(g * LANES, (g + 1) * LANES)
            s = s + (c_q[j][:, sl] - cb_sc[j * nkb + kb])
            if masked:
                s = jnp.where(causal[:, sl], s, -jnp.inf)
            m_new = jnp.maximum(m, jnp.max(s, axis=0, keepdims=True))
            alpha = jnp.exp(m - m_new)
            p = jnp.exp(s - m_new)
            l = alpha * l + jnp.sum(p, axis=0, keepdims=True)
            acc = alpha * acc + _mm(v_t, p.astype(BF16))
            out.append((m_new, l, acc))
        return tuple(out)

    def step(kb, state):
        return update(kb, scores(kb), state, False)

    init = (jnp.full((1, LANES), -jnp.inf, F32), jnp.zeros((1, LANES), F32), jnp.zeros((LANES, LANES), F32))
    state = lax.fori_loop(0, qi, step, (init,) * len(chains))
    fin = update(qi, scores(qi), state, True)
    head = [jnp.concatenate([fin[j * ng + g][2] / fin[j * ng + g][1] for g in range(ng)], axis=1) for j in range(2)]
    o_ref[...] = jnp.transpose(jnp.where(first, head[0], head[1])).astype(BF16)


def _fox(hc, c4, bsz, seq):
    tq = min(TQ_FOX, seq)
    nq = seq // tq
    pairs = FOX_HEADS // 2
    return pl.pallas_call(
        _fox_kernel,
        grid=(bsz, pairs, nq),
        in_specs=[pl.BlockSpec((tq, LANES), lambda bi, p, i: (bi * nq + i, p)),
                  pl.BlockSpec((seq, LANES), lambda bi, p, i: (bi, pairs + p)),
                  pl.BlockSpec((seq, LANES), lambda bi, p, i: (bi, 2 * pairs + p)),
                  pl.BlockSpec((None, 2, nq, tq), lambda bi, p, i: (p, 0, bi, 0))],
        out_specs=pl.BlockSpec((tq, LANES), lambda bi, p, i: (bi * nq + i, p)),
        out_shape=jax.ShapeDtypeStruct((bsz * seq, FOX_W), BF16),
        scratch_shapes=[pltpu.VMEM((nq, LANES, tq), BF16),
                        pltpu.VMEM((2 * nq, tq, LANES), F32)],
        compiler_params=_cparams("arbitrary", "arbitrary", "arbitrary"),
        name="fox_attention",
    )(hc, hc, hc, c4)


def _outproj_kernel(a_ref, b_ref, c_ref, x_ref, wa_ref, wb_ref, wc_ref, g_ref, beta_ref, o_ref):
    mix = _mm(a_ref[...], wa_ref[...]) + _mm(b_ref[...], wb_ref[...]) + _mm(c_ref[...], wc_ref[...])
    o_ref[...] = _layer_norm(DEEPNORM_ALPHA * x_ref[...] + mix, g_ref[...], beta_ref[...])


def _outproj(a, b, c, x, wa, wb, wc, g, beta):
    n = x.shape[0]
    tn = min(TN_PROJ, n)
    tok = lambda w: pl.BlockSpec((tn, w), lambda i: (i, 0))
    full = lambda r, w: pl.BlockSpec((r, w), lambda i: (0, 0))
    return pl.pallas_call(
        _outproj_kernel,
        grid=(n // tn,),
        in_specs=[tok(CONV_CH), tok(GLA_VW), tok(FOX_W), tok(D_MODEL),
                  full(CONV_CH, D_MODEL), full(GLA_VW, D_MODEL), full(FOX_W, D_MODEL),
                  full(1, D_MODEL), full(1, D_MODEL)],
        out_specs=tok(D_MODEL),
        out_shape=jax.ShapeDtypeStruct((n, D_MODEL), F32),
        compiler_params=_cparams("arbitrary"),
        name="outproj_ln",
    )(a, b, c, x, wa, wb, wc, g, beta)


def _topk_rows(sc, kk):
    nrow = sc.shape[0]
    rows = lax.broadcasted_iota(I32, sc.shape, 0).astype(F32)
    vals, idxs = [], []
    for _ in range(kk):
        m = jnp.max(sc, axis=0, keepdims=True)
        idx = jnp.min(jnp.where(sc == m, rows, float(nrow)), axis=0, keepdims=True)
        vals.append(m)
        idxs.append(idx)
        sc = jnp.where(rows == idx, -jnp.inf, sc)
    return jnp.concatenate(vals, axis=0), jnp.concatenate(idxs, axis=0)


def _take_rows(table, sel, kk):
    out = jnp.zeros_like(sel)
    for a in range(kk):
        out = jnp.where(sel == float(a), table[a:a + 1, :], out)
    return out


def _candidate_sums(sv1, sv2):
    half = PEER_TOPK // 2
    blocks = [sv1[0:1, :] + sv2]
    blocks += [sv1[a:a + 1, :] + sv2[0:half, :] for a in range(1, half)]
    blocks += [sv1[half:, :] + sv2[0:1, :]]
    cand = jnp.concatenate(blocks, axis=0)
    n0, n1 = PEER_TOPK, PEER_TOPK + half * (half - 1)

    def a_of(r):
        q = jnp.floor((r - n0) * (1.0 / half))
        return jnp.where(r < n0, 0.0, jnp.where(r < n1, 1.0 + q, r - (n1 - half)))

    def b_of(r):
        q = jnp.floor((r - n0) * (1.0 / half))
        return jnp.where(r < n0, r, jnp.where(r < n1, r - n0 - half * q, 0.0))

    return cand, a_of, b_of


PACK_ROWS = 512


def _pack_kernel(t_ref, o_ref):
    for j in range(PEER_WORDS):
        lo = t_ref[:, (2 * j) * LANES:(2 * j + 1) * LANES].astype(BF16).astype(F32)
        hi = t_ref[:, (2 * j + 1) * LANES:(2 * j + 2) * LANES].astype(BF16).astype(F32)
        lo_bits = lax.shift_right_logical(lax.bitcast_convert_type(lo, I32), jnp.int32(16))
        hi_bits = lax.bitcast_convert_type(hi, I32) & jnp.int32(-65536)
        o_ref[pl.ds(j, t_ref.shape[0], stride=PEER_WORDS), :] = hi_bits | lo_bits


def _pack_table(t):
    ne = t.shape[0]
    rows = min(PACK_ROWS, ne)
    return pl.pallas_call(
        _pack_kernel,
        grid=(ne // rows,),
        in_specs=[pl.BlockSpec((rows, D_MODEL), lambda i: (i, 0))],
        out_specs=pl.BlockSpec((rows * PEER_WORDS, LANES), lambda i: (i, 0)),
        out_shape=jax.ShapeDtypeStruct((ne * PEER_WORDS, LANES), I32),
        compiler_params=_cparams("arbitrary"),
        name="peer_pack",
    )(t)


def _gather_rows(idx_ref, tab_ref, stage, t):
    nsel = PEER_HEADS * PEER_TOPK
    for k in range(nsel):
        i = pl.multiple_of(idx_ref[0, 0, t * nsel + k], PEER_WORDS)
        stage[PEER_WORDS * k:PEER_WORDS * (k + 1), :] = tab_ref[pl.ds(i, PEER_WORDS), :]


def _unpack_rows(stage):
    w = stage[...]
    lo = lax.bitcast_convert_type(w << 16, F32).astype(BF16)
    hi = lax.bitcast_convert_type(w & jnp.int32(-65536), F32).astype(BF16)
    return lo, hi


def _token_pipeline(idx_ref, tab_ref, stages, tn, compute):
    _gather_rows(idx_ref, tab_ref, stages[0], 0)

    def group(i, carry):
        t0 = PEER_UNROLL * i
        for u in range(PEER_UNROLL):
            _gather_rows(idx_ref, tab_ref, stages[(u + 1) % 2], jnp.minimum(t0 + u + 1, tn - 1))
            compute(stages[u % 2], t0 + u)
        return carry

    lax.fori_loop(0, tn // PEER_UNROLL, group, 0)


def _diag_masks():
    r = lax.broadcasted_iota(I32, (SUBLANES, PEER_STAGE), 0)
    j = lax.broadcasted_iota(I32, (SUBLANES, PEER_STAGE), 1) & (PEER_WORDS - 1)
    return r == 2 * j, r == 2 * j + 1


def _route_scores(x_ref, wq_ref, keys_ref, sc_ref):
    xq = _mm(x_ref[...].astype(BF16), wq_ref[...])
    for hc in range(2 * PEER_HEADS):
        xq_hc = xq[:, hc * PEER_HALF:(hc + 1) * PEER_HALF].astype(BF16)
        sc_ref[hc] = _nt(keys_ref[hc], xq_hc)


def _route_head(h, sc_ref, e_sc, g_sc):
    sv1, si1 = _topk_rows(sc_ref[2 * h], PEER_TOPK)
    sv2, si2 = _topk_rows(sc_ref[2 * h + 1], PEER_TOPK)
    cand, a_sel, b_sel = _candidate_sums(sv1, sv2)
    cv, ci = _topk_rows(cand, PEER_TOPK)
    a_sel, b_sel = a_sel(ci), b_sel(ci)
    i1 = _take_rows(si1, a_sel, PEER_TOPK)
    i2 = _take_rows(si2, b_sel, PEER_TOPK)
    e = jnp.exp(cv - cv[0:1, :])
    e_sc[h] = i1 * float(PEER_NKEYS) + i2
    g_sc[h] = e / jnp.sum(e, axis=0, keepdims=True)


def _fused_kernel(xr_ref, xp_ref, wq_ref, keys_ref, tab_ref, eidx_ref, w_ref,
                  sc_ref, e_sc, g_sc, g_keep, idx_v, idx_s, stage_a, stage_b, hbuf, sem):
    step = pl.program_id(0)
    last = pl.num_programs(0) - 1
    tn = xp_ref.shape[0]
    nsel = PEER_HEADS * PEER_TOPK
    m_lo, m_hi = _diag_masks()
    stages = (stage_a, stage_b)
    index_copy = pltpu.make_async_copy(idx_v, idx_s, sem)

    _route_scores(xr_ref, wq_ref, keys_ref, sc_ref)

    def gather(stage, t):
        for k in range(nsel):
            i = pl.multiple_of(idx_s[t, k], PEER_WORDS)
            stage[PEER_WORDS * k:PEER_WORDS * (k + 1), :] = tab_ref[pl.ds(i, PEER_WORDS), :]

    def compute(stage, t):
        lo, hi = _unpack_rows(stage)
        x8 = xp_ref[t].astype(BF16)
        part = jnp.where(m_lo, _nt(x8, lo), 0.0) + jnp.where(m_hi, _nt(x8, hi), 0.0)
        hbuf[pl.ds(t, 1), :] = jnp.sum(part, axis=0, keepdims=True)

    @pl.when(step == 0)
    def _():
        def trip(h, carry):
            _route_head(h, sc_ref, e_sc, g_sc)
            return carry

        lax.fori_loop(0, PEER_HEADS, trip, 0)

    @pl.when(step > 0)
    def _():
        index_copy.wait()
        gather(stages[0], 0)

        def trip(h, carry):
            _route_head(h, sc_ref, e_sc, g_sc)
            t0 = PEER_UNROLL * h
            for u in range(PEER_UNROLL):
                gather(stages[(u + 1) % 2], jnp.minimum(t0 + u + 1, tn - 1))
                compute(stages[u % 2], t0 + u)
            return carry

        lax.fori_loop(0, PEER_HEADS, trip, 0)

        hsum = hbuf[...]
        n = lax.broadcasted_iota(I32, hsum.shape, 1)
        hsum = hsum + jnp.where((n & 1) == 0, pltpu.roll(hsum, PEER_STAGE - 1, 1), pltpu.roll(hsum, 1, 1))
        hsum = hsum + jnp.where((n & 2) == 0, pltpu.roll(hsum, PEER_STAGE - 2, 1), pltpu.roll(hsum, 2, 1))
        e_r = lax.broadcasted_iota(I32, (nsel, PEER_STAGE), 0)
        e_c = _div_pow2(lax.broadcasted_iota(I32, (nsel, PEER_STAGE), 1), PEER_WORDS)
        g4 = _mm(g_keep[...], jnp.where(e_r == e_c, 1.0, 0.0), precision=HIGHEST)
        w_ref[...] = g4 * jax.nn.gelu(hsum)

    e_all = e_sc[...].reshape(nsel, tn)
    g_all = g_sc[...].reshape(nsel, tn)
    eidx = jnp.transpose(e_all * float(PEER_WORDS)).astype(I32)
    eidx_ref[...] = eidx
    idx_v[...] = eidx
    g_keep[...] = jnp.transpose(g_all)

    @pl.when(step < last)
    def _():
        index_copy.start()


def _route_and_peer_u(x1, wq, keys, utab):
    n = x1.shape[0]
    tn = TN_FUSED
    assert n % tn == 0 and tn == PEER_HEADS * PEER_UNROLL
    nb = n // tn
    nsel = PEER_HEADS * PEER_TOPK
    qw = PEER_HEADS * 2 * PEER_HALF
    cur = lambda s: jnp.minimum(s, nb - 1)
    prev = lambda s: jnp.maximum(s - 1, 0)
    eidx, w4 = pl.pallas_call(
        _fused_kernel,
        grid=(nb + 1,),
        in_specs=[pl.BlockSpec((tn, D_MODEL), lambda s: (cur(s), 0)),
                  pl.BlockSpec((tn, SUBLANES, LANES), lambda s: (prev(s), 0, 0)),
                  pl.BlockSpec((D_MODEL, qw), lambda s: (0, 0), pipeline_mode=pl.Buffered(1)),
                  pl.BlockSpec((2 * PEER_HEADS, PEER_NKEYS, PEER_HALF), lambda s: (0, 0, 0)),
                  pl.BlockSpec((PEER_N * PEER_WORDS, LANES), lambda s: (0, 0), pipeline_mode=pl.Buffered(1))],
        out_specs=[pl.BlockSpec((tn, nsel), lambda s: (cur(s), 0)),
                   pl.BlockSpec((tn, PEER_STAGE), lambda s: (prev(s), 0))],
        out_shape=[jax.ShapeDtypeStruct((n, nsel), I32),
                   jax.ShapeDtypeStruct((n, PEER_STAGE), F32)],
        scratch_shapes=[pltpu.VMEM((2 * PEER_HEADS, PEER_NKEYS, tn), F32),
                        pltpu.VMEM((PEER_HEADS, PEER_TOPK, tn), F32),
                        pltpu.VMEM((PEER_HEADS, PEER_TOPK, tn), F32),
                        pltpu.VMEM((tn, nsel), F32),
                        pltpu.VMEM((tn, nsel), I32),
                        pltpu.SMEM((tn, nsel), I32),
                        pltpu.VMEM((PEER_STAGE, LANES), I32),
                        pltpu.VMEM((PEER_STAGE, LANES), I32),
                        pltpu.VMEM((tn, PEER_STAGE), F32),
                        pltpu.SemaphoreType.DMA],
        compiler_params=_cparams("arbitrary"),
        name="peer_route_u",
    )(x1, x1.reshape(n, SUBLANES, LANES), wq, keys, utab)
    return eidx, w4


def _peer_v_kernel(idx_ref, tab_ref, w_ref, o_ref, stage_a, stage_b):
    tn = w_ref.shape[0]
    m_lo, m_hi = _diag_masks()

    def compute(stage, t):
        lo, hi = _unpack_rows(stage)
        w_t = w_ref[pl.ds(t, 1), :]
        l_lo = jnp.where(m_lo, w_t, 0.0).astype(BF16)
        l_hi = jnp.where(m_hi, w_t, 0.0).astype(BF16)
        o_ref[t] = _mm(l_lo, lo) + _mm(l_hi, hi)

    _token_pipeline(idx_ref, tab_ref, (stage_a, stage_b), tn, compute)


def _peer_specs(n, tn):
    nsel = PEER_HEADS * PEER_TOPK
    idx_spec = pl.BlockSpec((1, 1, tn * nsel), lambda i: (i, 0, 0), memory_space=pltpu.SMEM)
    tab_spec = pl.BlockSpec((PEER_N * PEER_WORDS, LANES), lambda i: (0, 0), pipeline_mode=pl.Buffered(1))
    return idx_spec, tab_spec


def _peer_v(eidx3, vtab, w4):
    n = w4.shape[0]
    tn = min(TN_PEER, n)
    idx_spec, tab_spec = _peer_specs(n, tn)
    return pl.pallas_call(
        _peer_v_kernel,
        grid=(n // tn,),
        in_specs=[idx_spec, tab_spec,
                  pl.BlockSpec((tn, PEER_STAGE), lambda i: (i, 0))],
        out_specs=pl.BlockSpec((tn, SUBLANES, LANES), lambda i: (i, 0, 0)),
        out_shape=jax.ShapeDtypeStruct((n, SUBLANES, LANES), F32),
        scratch_shapes=[pltpu.VMEM((PEER_STAGE, LANES), I32),
                        pltpu.VMEM((PEER_STAGE, LANES), I32)],
        compiler_params=_cparams("arbitrary"),
        name="peer_v",
    )(eidx3, vtab, w4)


def _ple_kernel(x_ref, peer_ref, p_ref, gw_ref, gb_ref, pw_ref, g_ref, beta_ref, o_ref):
    r = DEEPNORM_ALPHA * x_ref[...] + peer_ref[...]
    gate = _sigmoid(_mm(r.astype(BF16), gw_ref[...]) + gb_ref[...])
    r = r + gate * _mm(p_ref[...].astype(BF16), pw_ref[...])
    o_ref[...] = _layer_norm(r, g_ref[...], beta_ref[...])


def _ple(x1, peer_out, p_all, layer, gw, gb, pw, g, beta):
    n = x1.shape[0]
    tn = min(TN_PROJ, n)
    nb = n // tn
    tok = lambda w: pl.BlockSpec((tn, w), lambda i: (i, 0))
    full = lambda r, w: pl.BlockSpec((r, w), lambda i: (0, 0))
    return pl.pallas_call(
        _ple_kernel,
        grid=(nb,),
        in_specs=[tok(D_MODEL), tok(D_MODEL),
                  pl.BlockSpec((tn, PLE_DIM), lambda i: (layer * nb + i, 0)),
                  full(D_MODEL, D_MODEL), full(1, D_MODEL), full(PLE_DIM, D_MODEL),
                  full(1, D_MODEL), full(1, D_MODEL)],
        out_specs=tok(D_MODEL),
        out_shape=jax.ShapeDtypeStruct((n, D_MODEL), F32),
        compiler_params=_cparams("arbitrary"),
        name="ple_ln",
    )(x1, peer_out, p_all, gw, gb, pw, g, beta)


def _pad_cols(w, width):
    return jnp.pad(w, ((0, 0), (0, width - w.shape[1])))


def _layout_w_in(w):
    qk = GLA_HEADS * GLA_DK
    o = 0
    a = w[:, o:o + A_W]; o += A_W
    bq = w[:, o:o + qk]; o += qk
    bk = w[:, o:o + qk]; o += qk
    bv = w[:, o:o + GLA_VW]; o += GLA_VW
    bg = w[:, o:o + GLA_VW]; o += GLA_VW
    blr = w[:, o:o + GLA_GATE_RANK]; o += GLA_GATE_RANK
    c = w[:, o:o + C_W]; o += C_W
    cf = w[:, o:o + FOX_HEADS]
    wp = jnp.concatenate([a, _pad_cols(bq, GLA_DKP), _pad_cols(bk, GLA_DKP), bv, bg,
                          _pad_cols(blr, LANES), c], axis=1).astype(BF16)
    wft = jnp.pad(cf.T, ((0, SUBLANES - FOX_HEADS), (0, 0)))
    return wp, wft


def _layer(x, p_all, layer, bsz, seq, prm):
    (w_in, conv_w, conv_b, conv_ln_g, conv_ln_b, gla_gate_w, gla_gate_b, gla_norm_g, fox_forget_b,
     w_out, ln1_g, ln1_b, peer_wq, peer_keys, peer_u, peer_v, ple_w, ple_gw, ple_gb, ln2_g, ln2_b) = prm
    n = bsz * seq
    row = lambda v: v.reshape(1, -1)

    wp, wft = _layout_w_in(w_in)
    ha, hb, hc, ft = _inproj(x, wp, wft)

    a_out = _conv_module(ha, conv_w, row(conv_b), row(conv_ln_g), row(conv_ln_b), bsz, seq)

    w2p = jnp.pad(gla_gate_w, ((0, LANES - GLA_GATE_RANK), (0, GLA_DKP - GLA_HEADS * GLA_DK))).astype(BF16)
    b2p = _pad_cols(row(gla_gate_b), GLA_DKP)
    b_out = _gla(hb, w2p, b2p, row(gla_norm_g), bsz, seq)

    bfp = jnp.pad(fox_forget_b, (0, SUBLANES - FOX_HEADS)).reshape(SUBLANES, 1)
    c = _fox_cumsum(ft, bfp, bsz, seq)
    tq = min(TQ_FOX, seq)
    c_out = _fox(hc, c.reshape(SUBLANES // 2, 2, n // tq, tq), bsz, seq)

    wo = w_out.astype(BF16)
    x1 = _outproj(a_out, b_out, c_out, x, wo[0:CONV_CH], wo[CONV_CH:CONV_CH + GLA_VW],
                  wo[CONV_CH + GLA_VW:], row(ln1_g), row(ln1_b))

    keys = peer_keys.reshape(2 * PEER_HEADS, PEER_NKEYS, PEER_HALF).astype(BF16)
    eidx, w4 = _route_and_peer_u(x1, peer_wq.astype(BF16), keys, _pack_table(peer_u))
    tn = min(TN_PEER, n)
    eidx3 = eidx.reshape(n // tn, 1, tn * PEER_HEADS * PEER_TOPK)
    peer_out = _peer_v(eidx3, _pack_table(peer_v), w4).reshape(n, D_MODEL)

    return _ple(x1, peer_out, p_all, layer, ple_gw.astype(BF16), row(ple_gb), ple_w.astype(BF16),
                row(ln2_g), row(ln2_b))


def kernel(x, p, w_in, conv_w, conv_b, conv_ln_g, conv_ln_b, gla_gate_w, gla_gate_b, gla_norm_g, fox_forget_b, w_out, ln1_g, ln1_b, peer_wq, peer_keys, peer_u, peer_v, ple_w, ple_gw, ple_gb, ln2_g, ln2_b):
    bsz, seq, d = x.shape
    depth = p.shape[0]
    xf = x.reshape(bsz * seq, d)
    p_all = p.reshape(depth * bsz * seq, PLE_DIM)
    stacked = (w_in, conv_w, conv_b, conv_ln_g, conv_ln_b, gla_gate_w, gla_gate_b, gla_norm_g, fox_forget_b,
               w_out, ln1_g, ln1_b, peer_wq, peer_keys, peer_u, peer_v, ple_w, ple_gw, ple_gb, ln2_g, ln2_b)
    for i in range(depth):
        xf = _layer(xf, p_all, i, bsz, seq, tuple(t[i] for t in stacked))
    return xf.reshape(bsz, seq, d)
```
